```python
import math
import jax, jax.numpy as jnp
from jax import lax
import numpy as np

D_MODEL = 1024
BATCH = 8
SEQ = 4096
DEPTH = 1

CONV_CH = D_MODEL
CONV_K = 31
RET_HEADS = D_MODEL // 256
RET_DK = 256
RET_DV = 2 * RET_DK
RET_CHUNK = 128
ROPE_BASE = 10000.0
FFN_DIM = 3 * D_MODEL
FFN_CONV_K = 3
EPS = 1e-6

W_GLU = 2 * CONV_CH
W_Q = RET_HEADS * RET_DK
W_K = RET_HEADS * RET_DK
W_V = RET_HEADS * RET_DV
W_G = RET_HEADS * RET_DV
W_GATES = 2 * D_MODEL
N_IN = W_GLU + W_Q + W_K + W_V + W_G + W_GATES
SPLITS = [W_GLU, W_GLU + W_Q, W_GLU + W_Q + W_K, W_GLU + W_Q + W_K + W_V,
          W_GLU + W_Q + W_K + W_V + W_G]

kernel_name = "hybrid_conformer_conv_retention_gated_block"


def rms_norm(x, g):
    xf = x.astype(jnp.float32)
    y = xf * lax.rsqrt(jnp.mean(xf * xf, axis=-1, keepdims=True) + EPS)
    return (y * g.astype(jnp.float32)).astype(x.dtype)


def layer_norm(x, g, b):
    xf = x.astype(jnp.float32)
    mu = jnp.mean(xf, axis=-1, keepdims=True)
    var = jnp.mean(jnp.square(xf - mu), axis=-1, keepdims=True)
    y = (xf - mu) * lax.rsqrt(var + EPS)
    return (y * g.astype(jnp.float32) + b.astype(jnp.float32)).astype(x.dtype)


def causal_depthwise_conv(x, w, b):
    K, C = w.shape
    y = lax.conv_general_dilated(
        x, w[:, None, :].astype(x.dtype), window_strides=(1,),
        padding=[(K - 1, 0)], dimension_numbers=("NWC", "WIO", "NWC"),
        feature_group_count=C)
    return y + b.astype(x.dtype)


def rotary(x, pos):
    half = x.shape[-1] // 2
    inv_freq = ROPE_BASE ** (-jnp.arange(half, dtype=jnp.float32) / half)
    ang = pos[:, None] * inv_freq[None, :]
    cos = jnp.cos(ang)[None, :, None, :].astype(x.dtype)
    sin = jnp.sin(ang)[None, :, None, :].astype(x.dtype)
    x1, x2 = x[..., :half], x[..., half:]
    return jnp.concatenate([x1 * cos - x2 * sin, x2 * cos + x1 * sin], axis=-1)


def retention_chunkwise(q, k, v):
    B, S, H, dk = q.shape
    dv = v.shape[-1]
    C = RET_CHUNK
    N = S // C
    log_gamma = jnp.log1p(-jnp.power(2.0, -5.0 - jnp.arange(H, dtype=jnp.float32)))
    idx = jnp.arange(C, dtype=jnp.float32)
    diff = idx[:, None] - idx[None, :]
    causal = diff >= 0
    inner_decay = jnp.where(causal[None],
                            jnp.exp(jnp.where(causal, diff, 0.0)[None] * log_gamma[:, None, None]),
                            0.0)
    cross_decay = jnp.exp((idx + 1.0)[None, :] * log_gamma[:, None])
    state_decay = jnp.exp((C - 1.0 - idx)[None, :] * log_gamma[:, None])
    chunk_decay = jnp.exp(C * log_gamma)

    def to_chunks(t):
        return t.reshape(B, N, C, H, t.shape[-1]).transpose(1, 0, 3, 2, 4)

    def step(R, qkv):
        qc, kc, vc = qkv
        s = jnp.einsum("bhqd,bhkd->bhqk", qc, kc) * inner_decay[None]
        inner = jnp.einsum("bhqk,bhkv->bhqv", s, vc)
        cross = jnp.einsum("bhqd,bhdv->bhqv", qc, R) * cross_decay[None, :, :, None]
        R_new = R * chunk_decay[None, :, None, None] + jnp.einsum(
            "bhkd,bhkv->bhdv", kc, vc * state_decay[None, :, :, None])
        return R_new, inner + cross

    R0 = jnp.zeros((B, H, dk, dv), jnp.float32)
    _, out = lax.scan(step, R0, (to_chunks(q), to_chunks(k), to_chunks(v)))
    return out.transpose(1, 0, 3, 2, 4).reshape(B, S, H, dv)


def setup_inputs(seed: int = 0) -> dict:
    key = jax.random.key(seed)
    ks = jax.random.split(key, 24)
    f32 = jnp.float32
    L = DEPTH

    def nrm(k, shape, scale):
        return jax.random.normal(k, shape, f32) * scale

    return {
        "x": jax.random.normal(ks[0], (BATCH, SEQ, D_MODEL), f32),
        "norm_mix_g": 1.0 + nrm(ks[1], (L, D_MODEL), 0.02),
        "w_in": nrm(ks[2], (L, D_MODEL, N_IN), D_MODEL ** -0.5),
        "gate_b": nrm(ks[3], (L, W_GATES), 0.02),
        "conv_dw_w": nrm(ks[4], (L, CONV_K, CONV_CH), CONV_K ** -0.5),
        "conv_dw_b": nrm(ks[5], (L, CONV_CH), 0.02),
        "conv_ln_g": 1.0 + nrm(ks[6], (L, CONV_CH), 0.02),
        "conv_ln_b": nrm(ks[7], (L, CONV_CH), 0.02),
        "w_conv_proj": nrm(ks[8], (L, CONV_CH, D_MODEL), CONV_CH ** -0.5),
        "conv_proj_b": nrm(ks[9], (L, D_MODEL), 0.02),
        "ret_norm_g": 1.0 + nrm(ks[10], (L, RET_HEADS * RET_DV), 0.02),
        "w_ret_proj": nrm(ks[11], (L, RET_HEADS * RET_DV, D_MODEL), (RET_HEADS * RET_DV) ** -0.5),
        "w_out": nrm(ks[12], (L, D_MODEL, D_MODEL), D_MODEL ** -0.5),
        "norm_ffn_g": 1.0 + nrm(ks[13], (L, D_MODEL), 0.02),
        "w_up": nrm(ks[14], (L, D_MODEL, 2 * FFN_DIM), D_MODEL ** -0.5),
        "ffn_dw_w": nrm(ks[15], (L, FFN_CONV_K, 2 * FFN_DIM), FFN_CONV_K ** -0.5),
        "ffn_dw_b": nrm(ks[16], (L, 2 * FFN_DIM), 0.02),
        "w_down": nrm(ks[17], (L, FFN_DIM, D_MODEL), FFN_DIM ** -0.5),
        "norm_final_g": 1.0 + nrm(ks[18], (D_MODEL,), 0.02),
    }


def reference(x, norm_mix_g, w_in, gate_b, conv_dw_w, conv_dw_b, conv_ln_g, conv_ln_b,
              w_conv_proj, conv_proj_b, ret_norm_g, w_ret_proj, w_out, norm_ffn_g,
              w_up, ffn_dw_w, ffn_dw_b, w_down, norm_final_g):
    B, S, D = x.shape
    pos = jnp.arange(S, dtype=jnp.float32)
    for l in range(DEPTH):
        h = rms_norm(x, norm_mix_g[l])
        proj = h @ w_in[l].astype(x.dtype)
        glu_in, q, k, v, g, gate_logits = jnp.split(proj, SPLITS, axis=-1)

        a = glu_in[..., :CONV_CH] * jax.nn.sigmoid(glu_in[..., CONV_CH:])
        a = causal_depthwise_conv(a, conv_dw_w[l], conv_dw_b[l])
        a = jax.nn.silu(layer_norm(a, conv_ln_g[l], conv_ln_b[l]))
        y_a = a @ w_conv_proj[l].astype(x.dtype) + conv_proj_b[l].astype(x.dtype)

        q = rotary(q.reshape(B, S, RET_HEADS, RET_DK), pos) * (RET_DK ** -0.5)
        k = rotary(k.reshape(B, S, RET_HEADS, RET_DK), pos)
        v = v.reshape(B, S, RET_HEADS, RET_DV)
        r = retention_chunkwise(q.astype(jnp.float32), k.astype(jnp.float32),
                                v.astype(jnp.float32))
        mu = jnp.mean(r, axis=-1, keepdims=True)
        var = jnp.mean(jnp.square(r - mu), axis=-1, keepdims=True)
        r = ((r - mu) * lax.rsqrt(var + EPS)).reshape(B, S, RET_HEADS * RET_DV)
        r = (r * ret_norm_g[l].astype(jnp.float32)).astype(x.dtype)
        r = jax.nn.silu(g) * r
        y_b = r @ w_ret_proj[l].astype(x.dtype)

        gates = jax.nn.sigmoid(gate_logits + gate_b[l].astype(x.dtype))
        g_a, g_b = gates[..., :D_MODEL], gates[..., D_MODEL:]
        mix = g_a * y_a + g_b * y_b
        x = x + mix @ w_out[l].astype(x.dtype)

        h2 = rms_norm(x, norm_ffn_g[l])
        u = h2 @ w_up[l].astype(x.dtype)
        u = causal_depthwise_conv(u, ffn_dw_w[l], ffn_dw_b[l])
        ff = jax.nn.silu(u[..., :FFN_DIM]) * u[..., FFN_DIM:]
        x = x + ff @ w_down[l].astype(x.dtype)
    return rms_norm(x, norm_final_g)
```

```python
import functools

import jax
import jax.numpy as jnp
from jax import lax
from jax.experimental import pallas as pl
from jax.experimental.pallas import tpu as pltpu

EPS = 1e-6
ROPE_BASE = 10000.0
RET_DK = 256
RET_DV = 2 * RET_DK
RET_CHUNK = 256
LANES = 128
SUBLANES = 8
MIB = 1024 * 1024
VMEM_LIMIT = 56 * MIB

BF16 = jnp.bfloat16
F32 = jnp.float32


def _sigmoid(x):
    return 1.0 / (1.0 + jnp.exp(-x))


def _silu(x):
    return x * _sigmoid(x)


def _resident(shape):
    nd = len(shape)
    return pl.BlockSpec(shape, lambda *_: (0,) * nd, pipeline_mode=pl.Buffered(1))


def _in_proj_body(x_ref, ng_ref, w_ref, gb_ref, cos_ref, sin_ref,
                  a_ref, q_ref, k_ref, v_ref, sg_ref, gt_ref, h_scr, *, d_model, heads):
    conv_ch = d_model
    off_q = 2 * conv_ch
    off_k = off_q + heads * RET_DK
    off_v = off_k + heads * RET_DK
    off_g = off_v + heads * RET_DV
    off_gt = off_g + heads * RET_DV

    x = x_ref[...]
    ms = jnp.mean(x * x, axis=-1, keepdims=True)
    h_scr[...] = (x * lax.rsqrt(ms + EPS) * ng_ref[...]).astype(BF16)

    def mm(c0, n):
        return jnp.dot(h_scr[...], w_ref[:, c0:c0 + n], preferred_element_type=F32)

    for c in range(0, conv_ch, 256):
        a_ref[:, c:c + 256] = (mm(c, 256) * _sigmoid(mm(conv_ch + c, 256))).astype(BF16)

    cos = cos_ref[...]
    sin = sin_ref[...]
    half = RET_DK // 2
    for out_ref, off, scale in ((q_ref, off_q, RET_DK ** -0.5), (k_ref, off_k, None)):
        for hd in range(heads):
            y = mm(off + hd * RET_DK, RET_DK)
            y1, y2 = y[:, :half], y[:, half:]
            r1 = y1 * cos - y2 * sin
            r2 = y2 * cos + y1 * sin
            if scale is not None:
                r1, r2 = r1 * scale, r2 * scale
            out_ref[:, hd * RET_DK:hd * RET_DK + half] = r1.astype(BF16)
            out_ref[:, hd * RET_DK + half:(hd + 1) * RET_DK] = r2.astype(BF16)

    for c in range(0, heads * RET_DV, 512):
        v_ref[:, c:c + 512] = mm(off_v + c, 512).astype(BF16)
    for c in range(0, heads * RET_DV, 512):
        sg_ref[:, c:c + 512] = _silu(mm(off_g + c, 512)).astype(BF16)
    for c in range(0, 2 * d_model, 512):
        gt_ref[:, c:c + 512] = _sigmoid(mm(off_gt + c, 512) + gb_ref[:, c:c + 512]).astype(BF16)


def _in_proj(x2, norm_g, w_in, gate_b, cos, sin, *, seq, heads, tm=256):
    tokens, d_model = x2.shape
    n_in = w_in.shape[1]
    half = RET_DK // 2
    s_tiles = seq // tm
    row = lambda i: (i, 0)
    widths = (d_model, heads * RET_DK, heads * RET_DK, heads * RET_DV, heads * RET_DV, 2 * d_model)
    return pl.pallas_call(
        functools.partial(_in_proj_body, d_model=d_model, heads=heads),
        grid=(tokens // tm,),
        in_specs=[
            pl.BlockSpec((tm, d_model), row),
            _resident((1, d_model)),
            _resident((d_model, n_in)),
            _resident((1, 2 * d_model)),
            pl.BlockSpec((tm, half), lambda i: (i % s_tiles, 0)),
            pl.BlockSpec((tm, half), lambda i: (i % s_tiles, 0)),
        ],
        out_specs=[pl.BlockSpec((tm, w), row) for w in widths],
        out_shape=[jax.ShapeDtypeStruct((tokens, w), BF16) for w in widths],
        scratch_shapes=[pltpu.VMEM((tm, d_model), BF16)],
        compiler_params=pltpu.CompilerParams(
            dimension_semantics=("arbitrary",), vmem_limit_bytes=VMEM_LIMIT),
        name="in_proj",
    )(x2, norm_g, w_in, gate_b, cos, sin)


def _retention_body(cdec_ref, q_ref, k_ref, v_ref, sg_ref, inner_ref, cross_ref, state_ref, rg_ref,
                    o_ref, r_scr, *, chunks):
    hd = pl.program_id(1)

    @pl.when(pl.program_id(2) == 0)
    def _():
        r_scr[...] = jnp.zeros_like(r_scr)

    chunk_decay = cdec_ref[hd]
    c_len = RET_CHUNK
    for c in range(chunks):
        rows = slice(c * c_len, (c + 1) * c_len)
        qc = q_ref[rows, :]
        kc = k_ref[rows, :]
        vc = v_ref[rows, :]
        state = r_scr[...]
        s = lax.dot_general(qc, kc, (((1,), (1,)), ((), ())), preferred_element_type=F32) * inner_ref[0]
        inner = jnp.dot(s.astype(BF16), vc, preferred_element_type=F32)
        cross = jnp.dot(qc, state.astype(BF16), preferred_element_type=F32) * cross_ref[0]
        kw = (kc.astype(F32) * state_ref[0]).astype(BF16)
        r_scr[...] = state * chunk_decay + lax.dot_general(
            kw, vc, (((0,), (0,)), ((), ())), preferred_element_type=F32)
        r = inner + cross
        mu = jnp.mean(r, axis=-1, keepdims=True)
        d = r - mu
        var = jnp.mean(d * d, axis=-1, keepdims=True)
        rn = d * lax.rsqrt(var + EPS) * rg_ref[...]
        o_ref[rows, :] = (sg_ref[rows, :].astype(F32) * rn).astype(BF16)


def _retention(q, k, v, sg, ret_norm_g, *, batch, seq, heads, ts=1024):
    tokens = q.shape[0]
    c_len = RET_CHUNK
    s_tiles = seq // ts
    log_gamma = jnp.log1p(-jnp.power(2.0, -5.0 - jnp.arange(heads, dtype=F32)))
    idx = jnp.arange(c_len, dtype=F32)
    diff = idx[:, None] - idx[None, :]
    causal = diff >= 0
    inner_decay = jnp.where(
        causal[None], jnp.exp(jnp.where(causal, diff, 0.0)[None] * log_gamma[:, None, None]), 0.0)
    cross_decay = jnp.exp((idx + 1.0)[None, :] * log_gamma[:, None])[:, :, None]
    state_decay = jnp.exp((c_len - 1.0 - idx)[None, :] * log_gamma[:, None])[:, :, None]
    chunk_decay = jnp.exp(c_len * log_gamma)

    tok = lambda b, h, s: (b * s_tiles + s, h)
    per_head = lambda b, h, s: (h, 0, 0)
    return pl.pallas_call(
        functools.partial(_retention_body, chunks=ts // c_len),
        grid=(batch, heads, s_tiles),
        in_specs=[
            pl.BlockSpec(memory_space=pltpu.SMEM),
            pl.BlockSpec((ts, RET_DK), tok),
            pl.BlockSpec((ts, RET_DK), tok),
            pl.BlockSpec((ts, RET_DV), tok),
            pl.BlockSpec((ts, RET_DV), tok),
            pl.BlockSpec((1, c_len, c_len), per_head),
            pl.BlockSpec((1, c_len, 1), per_head),
            pl.BlockSpec((1, c_len, 1), per_head),
            pl.BlockSpec((1, RET_DV), lambda b, h, s: (0, h)),
        ],
        out_specs=pl.BlockSpec((ts, RET_DV), tok),
        out_shape=jax.ShapeDtypeStruct((tokens, heads * RET_DV), BF16),
        scratch_shapes=[pltpu.VMEM((RET_DK, RET_DV), F32)],
        compiler_params=pltpu.CompilerParams(
            dimension_semantics=("arbitrary", "arbitrary", "arbitrary"), vmem_limit_bytes=VMEM_LIMIT),
        name="retention",
    )(chunk_decay, q, k, v, sg, inner_decay, cross_decay, state_decay, ret_norm_g)


def _conv_buf_shape(width, halo, rows):
    return (width // (2 * LANES), 2 * (halo + rows), LANES)


def _conv_buf_store(buf, halo, val):
    rows, width = val.shape
    for s in range(width // LANES):
        buf[s // 2, pl.ds(2 * halo + s % 2, rows, stride=2), :] = val[:, s * LANES:(s + 1) * LANES]


def _causal_dw_conv(buf, halo, w_ref, b_ref, dst_ref, *, rows, row_block):
    taps = w_ref.shape[0]
    width = dst_ref.shape[-1]
    base = halo - (taps - 1)

    def block(i, carry):
        r0 = pl.multiple_of(i * row_block, row_block)
        for s in range(width // LANES):
            lanes = slice(s * LANES, (s + 1) * LANES)
            acc = jnp.broadcast_to(b_ref[:, lanes], (row_block, LANES))
            for k in range(taps):
                src = buf[s // 2, pl.ds(2 * (r0 + base + k) + s % 2, row_block, stride=2), :]
                acc = acc + src * w_ref[k:k + 1, lanes]
            dst_ref[pl.ds(r0, row_block), lanes] = acc
        return carry

    lax.fori_loop(0, rows // row_block, block, 0)


CONV_HALO = 32


def _mix_out_body(x_ref, a_ref, r_ref, gt_ref, dww_ref, dwb_ref, lng_ref, lnb_ref,
                  wcp_ref, cpb_ref, wrp_ref, wout_ref, o_ref, abuf, cbuf, *, tm, d_model):
    @pl.when(pl.program_id(1) == 0)
    def _():
        abuf[:, 0:2 * CONV_HALO, :] = jnp.zeros((abuf.shape[0], 2 * CONV_HALO, LANES), F32)

    @pl.when(pl.program_id(1) > 0)
    def _():
        abuf[:, 0:2 * CONV_HALO, :] = abuf[:, 2 * tm:2 * (tm + CONV_HALO), :]

    _conv_buf_store(abuf, CONV_HALO, a_ref[...].astype(F32))
    _causal_dw_conv(abuf, CONV_HALO, dww_ref, dwb_ref, cbuf, rows=tm, row_block=32)

    c = cbuf[...]
    mu = jnp.mean(c, axis=-1, keepdims=True)
    d = c - mu
    var = jnp.mean(d * d, axis=-1, keepdims=True)
    act = _silu(d * lax.rsqrt(var + EPS) * lng_ref[...] + lnb_ref[...]).astype(BF16)
    y_a = jnp.dot(act, wcp_ref[...], preferred_element_type=F32) + cpb_ref[...]
    y_b = jnp.dot(r_ref[...], wrp_ref[...], preferred_element_type=F32)
    mix = (gt_ref[:, :d_model].astype(F32) * y_a + gt_ref[:, d_model:].astype(F32) * y_b).astype(BF16)
    o_ref[...] = x_ref[...] + jnp.dot(mix, wout_ref[...], preferred_element_type=F32)


def _mix_out(x2, a, r, gates, dw_w, dw_b, ln_g, ln_b, w_cp, cp_b, w_rp, w_out, *, batch, seq, tm=512):
    tokens, d_model = x2.shape
    s_tiles = seq // tm
    taps = dw_w.shape[0]
    assert taps - 1 <= CONV_HALO <= tm
    tok = lambda b, s: (b * s_tiles + s, 0)
    return pl.pallas_call(
        functools.partial(_mix_out_body, tm=tm, d_model=d_model),
        grid=(batch, s_tiles),
        in_specs=[
            pl.BlockSpec((tm, d_model), tok),
            pl.BlockSpec((tm, d_model), tok),
            pl.BlockSpec((tm, r.shape[1]), tok),
            pl.BlockSpec((tm, 2 * d_model), tok),
            _resident(dw_w.shape), _resident(dw_b.shape), _resident(ln_g.shape), _resident(ln_b.shape),
            _resident(w_cp.shape), _resident(cp_b.shape), _resident(w_rp.shape), _resident(w_out.shape),
        ],
        out_specs=pl.BlockSpec((tm, d_model), tok),
        out_shape=jax.ShapeDtypeStruct((tokens, d_model), F32),
        scratch_shapes=[pltpu.VMEM(_conv_buf_shape(d_model, CONV_HALO, tm), F32),
                        pltpu.VMEM((tm, d_model), F32)],
        compiler_params=pltpu.CompilerParams(
            dimension_semantics=("arbitrary", "arbitrary"), vmem_limit_bytes=VMEM_LIMIT),
        name="mix_out",
    )(x2, a, r, gates, dw_w, dw_b, ln_g, ln_b, w_cp, cp_b, w_rp, w_out)


FFN_HALO = SUBLANES


def _ffn_body(x_ref, ng_ref, wup_ref, dww_ref, dwb_ref, wdn_ref, fg_ref, o_ref,
              h_scr, carry, ubuf, cbuf, acc, *, tm, ffn_dim, fc, final_norm):
    @pl.when(pl.program_id(1) == 0)
    def _():
        carry[...] = jnp.zeros_like(carry)

    x = x_ref[...]
    ms = jnp.mean(x * x, axis=-1, keepdims=True)
    h_scr[...] = (x * lax.rsqrt(ms + EPS) * ng_ref[...]).astype(BF16)

    for j, c in enumerate(range(0, ffn_dim, fc)):
        for half_idx, off in enumerate((c, ffn_dim + c)):
            cols = slice(off, off + fc)
            ci = 2 * j + half_idx
            ubuf[:, 0:2 * FFN_HALO, :] = carry[ci]
            _conv_buf_store(ubuf, FFN_HALO, jnp.dot(h_scr[...], wup_ref[:, cols], preferred_element_type=F32))
            carry[ci] = ubuf[:, 2 * tm:2 * (tm + FFN_HALO), :]
            _causal_dw_conv(ubuf, FFN_HALO, dww_ref.at[:, cols], dwb_ref.at[:, cols], cbuf.at[half_idx],
                            rows=tm, row_block=64)
        ff = (_silu(cbuf[0]) * cbuf[1]).astype(BF16)
        part = jnp.dot(ff, wdn_ref[c:c + fc, :], preferred_element_type=F32)
        if j == 0:
            acc[...] = part
        else:
            acc[...] += part

    y = x_ref[...] + acc[...]
    if final_norm:
        ms = jnp.mean(y * y, axis=-1, keepdims=True)
        y = y * lax.rsqrt(ms + EPS) * fg_ref[...]
    o_ref[...] = y


def _ffn(x2, norm_g, w_up, dw_w, dw_b, w_down, final_g, *, batch, seq, final_norm, tm=512, fc=512):
    tokens, d_model = x2.shape
    ffn_dim = w_down.shape[0]
    s_tiles = seq // tm
    assert dw_w.shape[0] - 1 <= FFN_HALO
    tok = lambda b, s: (b * s_tiles + s, 0)
    return pl.pallas_call(
        functools.partial(_ffn_body, tm=tm, ffn_dim=ffn_dim, fc=fc, final_norm=final_norm),
        grid=(batch, s_tiles),
        in_specs=[
            pl.BlockSpec((tm, d_model), tok),
            _resident(norm_g.shape), _resident(w_up.shape), _resident(dw_w.shape), _resident(dw_b.shape),
            _resident(w_down.shape), _resident(final_g.shape),
        ],
        out_specs=pl.BlockSpec((tm, d_model), tok),
        out_shape=jax.ShapeDtypeStruct((tokens, d_model), F32),
        scratch_shapes=[
            pltpu.VMEM((tm, d_model), BF16),
            pltpu.VMEM((2 * ffn_dim // fc,) + _conv_buf_shape(fc, FFN_HALO, 0), F32),
            pltpu.VMEM(_conv_buf_shape(fc, FFN_HALO, tm), F32),
            pltpu.VMEM((2, tm, fc), F32),
            pltpu.VMEM((tm, d_model), F32),
        ],
        compiler_params=pltpu.CompilerParams(
            dimension_semantics=("arbitrary", "arbitrary"), vmem_limit_bytes=VMEM_LIMIT),
        name="ffn",
    )(x2, norm_g, w_up, dw_w, dw_b, w_down, final_g)


def kernel(x, norm_mix_g, w_in, gate_b, conv_dw_w, conv_dw_b, conv_ln_g, conv_ln_b, w_conv_proj,
           conv_proj_b, ret_norm_g, w_ret_proj, w_out, norm_ffn_g, w_up, ffn_dw_w, ffn_dw_b, w_down,
           norm_final_g):
    batch, seq, d_model = x.shape
    depth = w_in.shape[0]
    heads = d_model // RET_DK
    half = RET_DK // 2

    pos = jnp.arange(seq, dtype=F32)
    inv_freq = ROPE_BASE ** (-jnp.arange(half, dtype=F32) / half)
    ang = pos[:, None] * inv_freq[None, :]
    cos, sin = jnp.cos(ang), jnp.sin(ang)

    row = lambda v: v.reshape(1, -1)
    x2 = x.reshape(batch * seq, d_model)
    for l in range(depth):
        a, q, k, v, sg, gates = _in_proj(
            x2, row(norm_mix_g[l]), w_in[l].astype(BF16), row(gate_b[l]), cos, sin, seq=seq, heads=heads)
        r = _retention(q, k, v, sg, row(ret_norm_g[l]), batch=batch, seq=seq, heads=heads)
        x2 = _mix_out(
            x2, a, r, gates, conv_dw_w[l], row(conv_dw_b[l]), row(conv_ln_g[l]), row(conv_ln_b[l]),
            w_conv_proj[l].astype(BF16), row(conv_proj_b[l]), w_ret_proj[l].astype(BF16),
            w_out[l].astype(BF16), batch=batch, seq=seq)
        x2 = _ffn(
            x2, row(norm_ffn_g[l]), w_up[l].astype(BF16), ffn_dw_w[l], row(ffn_dw_b[l]),
            w_down[l].astype(BF16), row(norm_final_g), batch=batch, seq=seq,
            final_norm=(l == depth - 1))
    return x2.reshape(batch, seq, d_model)
```

```python
import functools

import jax
import jax.numpy as jnp
from jax import lax
from jax.experimental import pallas as pl
from jax.experimental.pallas import tpu as pltpu

EPS = 1e-6
ROPE_BASE = 10000.0
RET_DK = 256
RET_DV = 2 * RET_DK
RET_CHUNK = 256
LANES = 128
SUBLANES = 8
MIB = 1024 * 1024
VMEM_LIMIT = 56 * MIB

BF16 = jnp.bfloat16
F32 = jnp.float32


def _sigmoid(x):
    return 1.0 / (1.0 + jnp.exp(-x))


def _silu(x):
    return x * _sigmoid(x)


def _resident(shape):
    nd = len(shape)
    return pl.BlockSpec(shape, lambda *_: (0,) * nd, pipeline_mode=pl.Buffered(1))


def _in_proj_body(x_ref, ng_ref, w_ref, gb_ref, cos_ref, sin_ref,
                  a_ref, q_ref, k_ref, v_ref, sg_ref, gt_ref, h_scr, *, d_model, heads):
    conv_ch = d_model
    off_q = 2 * conv_ch
    off_k = off_q + heads * RET_DK
    off_v = off_k + heads * RET_DK
    off_g = off_v + heads * RET_DV
    off_gt = off_g + heads * RET_DV

    x = x_ref[...]
    ms = jnp.mean(x * x, axis=-1, keepdims=True)
    h_scr[...] = (x * lax.rsqrt(ms + EPS) * ng_ref[...]).astype(BF16)

    def mm(c0, n):
        return jnp.dot(h_scr[...], w_ref[:, c0:c0 + n], preferred_element_type=F32)

    for c in range(0, conv_ch, 256):
        a_ref[:, c:c + 256] = (mm(c, 256) * _sigmoid(mm(conv_ch + c, 256))).astype(BF16)

    cos = cos_ref[...]
    sin = sin_ref[...]
    half = RET_DK // 2
    for out_ref, off, scale in ((q_ref, off_q, RET_DK ** -0.5), (k_ref, off_k, None)):
        for hd in range(heads):
            y = mm(off + hd * RET_DK, RET_DK)
            y1, y2 = y[:, :half], y[:, half:]
            r1 = y1 * cos - y2 * sin
            r2 = y2 * cos + y1 * sin
            if scale is not None:
                r1, r2 = r1 * scale, r2 * scale
            out_ref[:, hd * RET_DK:hd * RET_DK + half] = r1.astype(BF16)
            out_ref[:, hd * RET_DK + half:(hd + 1) * RET_DK] = r2.astype(BF16)

    for c in range(0, heads * RET_DV, 512):
        v_ref[:, c:c + 512] = mm(off_v + c, 512).astype(BF16)
    for c in range(0, heads * RET_DV, 512):
        sg_ref[:, c:c + 512] = _silu(mm(off_g + c, 512)).astype(BF16)
    for c in range(0, 2 * d_model, 512):
        gt_ref[:, c:c + 512] = _sigmoid(mm(off_gt + c, 512) + gb_ref[:, c:c + 512]).astype(BF16)


def _in_proj(x2, norm_g, w_in, gate_b, cos, sin, *, seq, heads, tm=256):
    tokens, d_model = x2.shape
    n_in = w_in.shape[1]
    half = RET_DK // 2
    s_tiles = seq // tm
    row = lambda i: (i, 0)
    widths = (d_model, heads * RET_DK, heads * RET_DK, heads * RET_DV, heads * RET_DV, 2 * d_model)
    return pl.pallas_call(
        functools.partial(_in_proj_body, d_model=d_model, heads=heads),
        grid=(tokens // tm,),
        in_specs=[
            pl.BlockSpec((tm, d_model), row),
            _resident((1, d_model)),
            _resident((d_model, n_in)),
            _resident((1, 2 * d_model)),
            pl.BlockSpec((tm, half), lambda i: (i % s_tiles, 0)),
            pl.BlockSpec((tm, half), lambda i: (i % s_tiles, 0)),
        ],
        out_specs=[pl.BlockSpec((tm, w), row) for w in widths],
        out_shape=[jax.ShapeDtypeStruct((tokens, w), BF16) for w in widths],
        scratch_shapes=[pltpu.VMEM((tm, d_model), BF16)],
        compiler_params=pltpu.CompilerParams(
            dimension_semantics=("arbitrary",), vmem_limit_bytes=VMEM_LIMIT),
        name="in_proj",
    )(x2, norm_g, w_in, gate_b, cos, sin)


def _retention_body(cdec_ref, q_ref, k_ref, v_ref, sg_ref, inner_ref, cross_ref, state_ref, rg_ref,
                    o_ref, r_scr, *, chunks):
    hd = pl.program_id(1)

    @pl.when(pl.program_id(2) == 0)
    def _():
        r_scr[...] = jnp.zeros_like(r_scr)

    chunk_decay = cdec_ref[hd]
    c_len = RET_CHUNK
    for c in range(chunks):
        rows = slice(c * c_len, (c + 1) * c_len)
        qc = q_ref[rows, :]
        kc = k_ref[rows, :]
        vc = v_ref[rows, :]
        state = r_scr[...]
        s = lax.dot_general(qc, kc, (((1,), (1,)), ((), ())), preferred_element_type=F32) * inner_ref[0]
        inner = jnp.dot(s.astype(BF16), vc, preferred_element_type=F32)
        cross = jnp.dot(qc, state.astype(BF16), preferred_element_type=F32) * cross_ref[0]
        kw = (kc.astype(F32) * state_ref[0]).astype(BF16)
        r_scr[...] = state * chunk_decay + lax.dot_general(
            kw, vc, (((0,), (0,)), ((), ())), preferred_element_type=F32)
        r = inner + cross
        mu = jnp.mean(r, axis=-1, keepdims=True)
        d = r - mu
        var = jnp.mean(d * d, axis=-1, keepdims=True)
        rn = d * lax.rsqrt(var + EPS) * rg_ref[...]
        o_ref[rows, :] = (sg_ref[rows, :].astype(F32) * rn).astype(BF16)


def _retention(q, k, v, sg, ret_norm_g, *, batch, seq, heads, ts=1024):
    tokens = q.shape[0]
    c_len = RET_CHUNK
    s_tiles = seq // ts
    log_gamma = jnp.log1p(-jnp.power(2.0, -5.0 - jnp.arange(heads, dtype=F32)))
    idx = jnp.arange(c_len, dtype=F32)
    diff = idx[:, None] - idx[None, :]
    causal = diff >= 0
    inner_decay = jnp.where(
        causal[None], jnp.exp(jnp.where(causal, diff, 0.0)[None] * log_gamma[:, None, None]), 0.0)
    cross_decay = jnp.exp((idx + 1.0)[None, :] * log_gamma[:, None])[:, :, None]
    state_decay = jnp.exp((c_len - 1.0 - idx)[None, :] * log_gamma[:, None])[:, :, None]
    chunk_decay = jnp.exp(c_len * log_gamma)

    tok = lambda b, h, s: (b * s_tiles + s, h)
    per_head = lambda b, h, s: (h, 0, 0)
    return pl.pallas_call(
        functools.partial(_retention_body, chunks=ts // c_len),
        grid=(batch, heads, s_tiles),
        in_specs=[
            pl.BlockSpec(memory_space=pltpu.SMEM),
            pl.BlockSpec((ts, RET_DK), tok),
            pl.BlockSpec((ts, RET_DK), tok),
            pl.BlockSpec((ts, RET_DV), tok),
            pl.BlockSpec((ts, RET_DV), tok),
            pl.BlockSpec((1, c_len, c_len), per_head),
            pl.BlockSpec((1, c_len, 1), per_head),
            pl.BlockSpec((1, c_len, 1), per_head),
            pl.BlockSpec((1, RET_DV), lambda b, h, s: (0, h)),
        ],
        out_specs=pl.BlockSpec((ts, RET_DV), tok),
        out_shape=jax.ShapeDtypeStruct((tokens, heads * RET_DV), BF16),
        scratch_shapes=[pltpu.VMEM((RET_DK, RET_DV), F32)],
        compiler_params=pltpu.CompilerParams(
            dimension_semantics=("arbitrary", "arbitrary", "arbitrary"), vmem_limit_bytes=VMEM_LIMIT),
        name="retention",
    )(chunk_decay, q, k, v, sg, inner_decay, cross_decay, state_decay, ret_norm_g)


def _conv_buf_shape(width, halo, rows):
    return (width // (2 * LANES), 2 * (halo + rows), LANES)


def _conv_buf_store(buf, halo, val):
    rows, width = val.shape
    for s in range(width // LANES):
        buf[s // 2, pl.ds(2 * halo + s % 2, rows, stride=2), :] = val[:, s * LANES:(s + 1) * LANES]


def _causal_dw_conv(buf, halo, w_ref, b_ref, dst_ref, *, rows, row_block):
    taps = w_ref.shape[0]
    width = dst_ref.shape[-1]
    base = halo - (taps - 1)

    def block(i, carry):
        r0 = pl.multiple_of(i * row_block, row_block)
        for s in range(width // LANES):
            lanes = slice(s * LANES, (s + 1) * LANES)
            acc = jnp.broadcast_to(b_ref[:, lanes], (row_block, LANES))
            for k in range(taps):
                src = buf[s // 2, pl.ds(2 * (r0 + base + k) + s % 2, row_block, stride=2), :]
                acc = acc + src * w_ref[k:k + 1, lanes]
            dst_ref[pl.ds(r0, row_block), lanes] = acc
        return carry

    lax.fori_loop(0, rows // row_block, block, 0)


CONV_HALO = 32


def _mix_out_body(x_ref, a_ref, r_ref, gt_ref, dww_ref, dwb_ref, lng_ref, lnb_ref,
                  wcp_ref, cpb_ref, wrp_ref, wout_ref, o_ref, abuf, cbuf, *, tm, d_model):
    @pl.when(pl.program_id(1) == 0)
    def _():
        abuf[:, 0:2 * CONV_HALO, :] = jnp.zeros((abuf.shape[0], 2 * CONV_HALO, LANES), F32)

    @pl.when(pl.program_id(1) > 0)
    def _():
        abuf[:, 0:2 * CONV_HALO, :] = abuf[:, 2 * tm:2 * (tm + CONV_HALO), :]

    _conv_buf_store(abuf, CONV_HALO, a_ref[...].astype(F32))
    _causal_dw_conv(abuf, CONV_HALO, dww_ref, dwb_ref, cbuf, rows=tm, row_block=32)

    c = cbuf[...]
    mu = jnp.mean(c, axis=-1, keepdims=True)
    d = c - mu
    var = jnp.mean(d * d, axis=-1, keepdims=True)
    act = _silu(d * lax.rsqrt(var + EPS) * lng_ref[...] + lnb_ref[...]).astype(BF16)
    y_a = jnp.dot(act, wcp_ref[...], preferred_element_type=F32) + cpb_ref[...]
    y_b = jnp.dot(r_ref[...], wrp_ref[...], preferred_element_type=F32)
    mix = (gt_ref[:, :d_model].astype(F32) * y_a + gt_ref[:, d_model:].astype(F32) * y_b).astype(BF16)
    o_ref[...] = x_ref[...] + jnp.dot(mix, wout_ref[...], preferred_element_type=F32)


def _mix_out(x2, a, r, gates, dw_w, dw_b, ln_g, ln_b, w_cp, cp_b, w_rp, w_out, *, batch, seq, tm=512):
    tokens, d_model = x2.shape
    s_tiles = seq // tm
    taps = dw_w.shape[0]
    assert taps - 1 <= CONV_HALO <= tm
    tok = lambda b, s: (b * s_tiles + s, 0)
    return pl.pallas_call(
        functools.partial(_mix_out_body, tm=tm, d_model=d_model),
        grid=(batch, s_tiles),
        in_specs=[
            pl.BlockSpec((tm, d_model), tok),
            pl.BlockSpec((tm, d_model), tok),
            pl.BlockSpec((tm, r.shape[1]), tok),
            pl.BlockSpec((tm, 2 * d_model), tok),
            _resident(dw_w.shape), _resident(dw_b.shape), _resident(ln_g.shape), _resident(ln_b.shape),
            _resident(w_cp.shape), _resident(cp_b.shape), _resident(w_rp.shape), _resident(w_out.shape),
        ],
        out_specs=pl.BlockSpec((tm, d_model), tok),
        out_shape=jax.ShapeDtypeStruct((tokens, d_model), F32),
        scratch_shapes=[pltpu.VMEM(_conv_buf_shape(d_model, CONV_HALO, tm), F32),
                        pltpu.VMEM((tm, d_model), F32)],
        compiler_params=pltpu.CompilerParams(
            dimension_semantics=("arbitrary", "arbitrary"), vmem_limit_bytes=VMEM_LIMIT),
        name="mix_out",
    )(x2, a, r, gates, dw_w, dw_b, ln_g, ln_b, w_cp, cp_b, w_rp, w_out)


FFN_HALO = SUBLANES


def _ffn_body(x_ref, ng_ref, wup_ref, dww_ref, dwb_ref, wdn_ref, fg_ref, o_ref,
              h_scr, carry, ubuf, ff_scr, *, tm, ffn_dim, fc, row_block, final_norm):
    @pl.when(pl.program_id(1) == 0)
    def _():
        carry[...] = jnp.zeros_like(carry)

    x = x_ref[...]
    ms = jnp.mean(x * x, axis=-1, keepdims=True)
    h_scr[...] = (x * lax.rsqrt(ms + EPS) * ng_ref[...]).astype(BF16)

    taps = dww_ref.shape[0]
    base = FFN_HALO - (taps - 1)
    for j, c in enumerate(range(0, ffn_dim, fc)):
        for half_idx, off in enumerate((c, ffn_dim + c)):
            ub = ubuf.at[j % 2, half_idx]
            ci = 2 * j + half_idx
            ub[:, 0:2 * FFN_HALO, :] = carry[ci]
            _conv_buf_store(ub, FFN_HALO,
                            jnp.dot(h_scr[...], wup_ref[:, off:off + fc], preferred_element_type=F32))
            carry[ci] = ub[:, 2 * tm:2 * (tm + FFN_HALO), :]
        for r0 in range(0, tm, row_block):
            for s in range(fc // LANES):
                conv = []
                for half_idx, off in enumerate((c, ffn_dim + c)):
                    lanes = slice(off + s * LANES, off + (s + 1) * LANES)
                    acc = jnp.broadcast_to(dwb_ref[:, lanes], (row_block, LANES))
                    for k in range(taps):
                        src = ubuf[j % 2, half_idx, s // 2,
                                   pl.ds(2 * (r0 + base + k) + s % 2, row_block, stride=2), :]
                        acc = acc + src * dww_ref[k:k + 1, lanes]
                    conv.append(acc)
                ff_scr[r0:r0 + row_block, c + s * LANES:c + (s + 1) * LANES] = (
                    _silu(conv[0]) * conv[1]).astype(BF16)

    y = x_ref[...] + jnp.dot(ff_scr[...], wdn_ref[...], preferred_element_type=F32)
    if final_norm:
        ms = jnp.mean(y * y, axis=-1, keepdims=True)
        y = y * lax.rsqrt(ms + EPS) * fg_ref[...]
    o_ref[...] = y


def _ffn(x2, norm_g, w_up, dw_w, dw_b, w_down, final_g, *, batch, seq, final_norm, tm=512, fc=512):
    tokens, d_model = x2.shape
    ffn_dim = w_down.shape[0]
    s_tiles = seq // tm
    assert dw_w.shape[0] - 1 <= FFN_HALO
    tok = lambda b, s: (b * s_tiles + s, 0)
    return pl.pallas_call(
        functools.partial(_ffn_body, tm=tm, ffn_dim=ffn_dim, fc=fc, row_block=64, final_norm=final_norm),
        grid=(batch, s_tiles),
        in_specs=[
            pl.BlockSpec((tm, d_model), tok),
            _resident(norm_g.shape), _resident(w_up.shape), _resident(dw_w.shape), _resident(dw_b.shape),
            _resident(w_down.shape), _resident(final_g.shape),
        ],
        out_specs=pl.BlockSpec((tm, d_model), tok),
        out_shape=jax.ShapeDtypeStruct((tokens, d_model), F32),
        scratch_shapes=[
            pltpu.VMEM((tm, d_model), BF16),
            pltpu.VMEM((2 * ffn_dim // fc,) + _conv_buf_shape(fc, FFN_HALO, 0), F32),
            pltpu.VMEM((2, 2) + _conv_buf_shape(fc, FFN_HALO, tm), F32),
            pltpu.VMEM((tm, ffn_dim), BF16),
        ],
        compiler_params=pltpu.CompilerParams(
            dimension_semantics=("arbitrary", "arbitrary"), vmem_limit_bytes=VMEM_LIMIT),
        name="ffn",
    )(x2, norm_g, w_up, dw_w, dw_b, w_down, final_g)


def kernel(x, norm_mix_g, w_in, gate_b, conv_dw_w, conv_dw_b, conv_ln_g, conv_ln_b, w_conv_proj,
           conv_proj_b, ret_norm_g, w_ret_proj, w_out, norm_ffn_g, w_up, ffn_dw_w, ffn_dw_b, w_down,
           norm_final_g):
    batch, seq, d_model = x.shape
    depth = w_in.shape[0]
    heads = d_model // RET_DK
    half = RET_DK // 2

    pos = jnp.arange(seq, dtype=F32)
    inv_freq = ROPE_BASE ** (-jnp.arange(half, dtype=F32) / half)
    ang = pos[:, None] * inv_freq[None, :]
    cos, sin = jnp.cos(ang), jnp.sin(ang)

    row = lambda v: v.reshape(1, -1)
    x2 = x.reshape(batch * seq, d_model)
    for l in range(depth):
        a, q, k, v, sg, gates = _in_proj(
            x2, row(norm_mix_g[l]), w_in[l].astype(BF16), row(gate_b[l]), cos, sin, seq=seq, heads=heads)
        r = _retention(q, k, v, sg, row(ret_norm_g[l]), batch=batch, seq=seq, heads=heads)
        x2 = _mix_out(
            x2, a, r, gates, conv_dw_w[l], row(conv_dw_b[l]), row(conv_ln_g[l]), row(conv_ln_b[l]),
            w_conv_proj[l].astype(BF16), row(conv_proj_b[l]), w_ret_proj[l].astype(BF16),
            w_out[l].astype(BF16), batch=batch, seq=seq)
        x2 = _ffn(
            x2, row(norm_ffn_g[l]), w_up[l].astype(BF16), ffn_dw_w[l], row(ffn_dw_b[l]),
            w_down[l].astype(BF16), row(norm_final_g), batch=batch, seq=seq,
            final_norm=(l == depth - 1))
    return x2.reshape(batch, seq, d_model)
```

```python
import functools

import jax
import jax.numpy as jnp
from jax import lax
from jax.experimental import pallas as pl
from jax.experimental.pallas import tpu as pltpu

EPS = 1e-6
ROPE_BASE = 10000.0
RET_DK = 256
RET_DV = 2 * RET_DK
RET_CHUNK = 256
LANES = 128
SUBLANES = 8
BF16_ROWS = 2 * SUBLANES
MIB = 1024 * 1024
VMEM_LIMIT = 56 * MIB

BF16 = jnp.bfloat16
F32 = jnp.float32


def _sigmoid(x):
    return 1.0 / (1.0 + jnp.exp(-x))


def _silu(x):
    return x * _sigmoid(x)


def _rms_norm(x, g):
    ms = jnp.mean(x * x, axis=-1, keepdims=True)
    return x * lax.rsqrt(ms + EPS) * g


def _resident(shape):
    nd = len(shape)
    return pl.BlockSpec(shape, lambda *_: (0,) * nd, pipeline_mode=pl.Buffered(1))


def _conv_buf_shape(width, halo, rows):
    return (width // (2 * LANES), 2 * (halo + rows), LANES)


def _conv_buf_store(buf, halo, val, slab0=0):
    rows, width = val.shape
    for i in range(width // LANES):
        s = slab0 + i
        buf[s // 2, pl.ds(2 * halo + s % 2, rows, stride=2), :] = val[:, i * LANES:(i + 1) * LANES]


def _conv_block(buf, halo, w_ref, b_ref, r0, rows, slab, col0):
    taps = w_ref.shape[0]
    base = halo - (taps - 1)
    lanes = slice(col0, col0 + LANES)
    acc = jnp.broadcast_to(b_ref[:, lanes], (rows, LANES))
    for k in range(taps):
        src = buf[slab // 2, pl.ds(2 * (r0 + base + k) + slab % 2, rows, stride=2), :]
        acc = acc + src * w_ref[k:k + 1, lanes]
    return acc


def _conv_halo_shift(buf, halo, rows, first):
    @pl.when(first)
    def _():
        buf[:, 0:2 * halo, :] = jnp.zeros((buf.shape[0], 2 * halo, LANES), F32)

    @pl.when(jnp.logical_not(first))
    def _():
        buf[:, 0:2 * halo, :] = buf[:, 2 * rows:2 * (rows + halo), :]


CONV_HALO = 32
CONV_ROWS = 64


def _mixer_in_body(x_ref, ng_ref, w_ref, gb_ref, cos_ref, sin_ref, dww_ref, dwb_ref, lng_ref, lnb_ref,
                   wcp_ref, cpb_ref, q_ref, k_ref, v_ref, sg_ref, gya_ref, gtb_ref,
                   h_scr, abuf, cbuf, act_scr, *, tm, d_model, heads):
    conv_ch = d_model
    off_q = 2 * conv_ch
    off_k = off_q + heads * RET_DK
    off_v = off_k + heads * RET_DK
    off_g = off_v + heads * RET_DV
    off_gt = off_g + heads * RET_DV

    _conv_halo_shift(abuf, CONV_HALO, tm, pl.program_id(1) == 0)
    h_scr[...] = _rms_norm(x_ref[...], ng_ref[...]).astype(BF16)

    def mm(c0, n):
        return jnp.dot(h_scr[...], w_ref[:, c0:c0 + n], preferred_element_type=F32)

    for c in range(0, conv_ch, 256):
        _conv_buf_store(abuf, CONV_HALO, mm(c, 256) * _sigmoid(mm(conv_ch + c, 256)), slab0=c // LANES)

    cos = cos_ref[...]
    sin = sin_ref[...]
    half = RET_DK // 2
    for out_ref, off, scale in ((q_ref, off_q, RET_DK ** -0.5), (k_ref, off_k, None)):
        for hd in range(heads):
            y = mm(off + hd * RET_DK, RET_DK)
            y1, y2 = y[:, :half], y[:, half:]
            r1 = y1 * cos - y2 * sin
            r2 = y2 * cos + y1 * sin
            if scale is not None:
                r1, r2 = r1 * scale, r2 * scale
            out_ref[:, hd * RET_DK:hd * RET_DK + half] = r1.astype(BF16)
            out_ref[:, hd * RET_DK + half:(hd + 1) * RET_DK] = r2.astype(BF16)

    for c in range(0, heads * RET_DV, 512):
        v_ref[:, c:c + 512] = mm(off_v + c, 512).astype(BF16)
    for c in range(0, heads * RET_DV, 512):
        sg_ref[:, c:c + 512] = _silu(mm(off_g + c, 512)).astype(BF16)
    for c in range(0, d_model, 512):
        gtb_ref[:, c:c + 512] = _sigmoid(
            mm(off_gt + d_model + c, 512) + gb_ref[:, d_model + c:d_model + c + 512]).astype(BF16)

    for r0 in range(0, tm, CONV_ROWS):
        for s in range(conv_ch // LANES):
            cbuf[r0:r0 + CONV_ROWS, s * LANES:(s + 1) * LANES] = _conv_block(
                abuf, CONV_HALO, dww_ref, dwb_ref, r0, CONV_ROWS, s, s * LANES)
    for r0 in range(0, tm, BF16_ROWS):
        c = cbuf[r0:r0 + BF16_ROWS, :]
        mu = jnp.mean(c, axis=-1, keepdims=True)
        d = c - mu
        var = jnp.mean(d * d, axis=-1, keepdims=True)
        act_scr[r0:r0 + BF16_ROWS, :] = _silu(
            d * lax.rsqrt(var + EPS) * lng_ref[...] + lnb_ref[...]).astype(BF16)

    for c in range(0, d_model, 512):
        y_a = jnp.dot(act_scr[...], wcp_ref[:, c:c + 512], preferred_element_type=F32) + cpb_ref[:, c:c + 512]
        g_a = _sigmoid(mm(off_gt + c, 512) + gb_ref[:, c:c + 512])
        gya_ref[:, c:c + 512] = (g_a * y_a).astype(BF16)


def _mixer_in(x2, norm_g, w_in, gate_b, cos, sin, dw_w, dw_b, ln_g, ln_b, w_cp, cp_b,
              *, batch, seq, heads, tm=512):
    tokens, d_model = x2.shape
    half = RET_DK // 2
    s_tiles = seq // tm
    assert dw_w.shape[0] - 1 <= CONV_HALO <= tm
    tok = lambda b, s: (b * s_tiles + s, 0)
    pos = lambda b, s: (s, 0)
    widths = (heads * RET_DK, heads * RET_DK, heads * RET_DV, heads * RET_DV, d_model, d_model)
    return pl.pallas_call(
        functools.partial(_mixer_in_body, tm=tm, d_model=d_model, heads=heads),
        grid=(batch, s_tiles),
        in_specs=[
            pl.BlockSpec((tm, d_model), tok),
            _resident(norm_g.shape), _resident(w_in.shape), _resident(gate_b.shape),
            pl.BlockSpec((tm, half), pos), pl.BlockSpec((tm, half), pos),
            _resident(dw_w.shape), _resident(dw_b.shape), _resident(ln_g.shape), _resident(ln_b.shape),
            _resident(w_cp.shape), _resident(cp_b.shape),
        ],
        out_specs=[pl.BlockSpec((tm, w), tok) for w in widths],
        out_shape=[jax.ShapeDtypeStruct((tokens, w), BF16) for w in widths],
        scratch_shapes=[
            pltpu.VMEM((tm, d_model), BF16),
            pltpu.VMEM(_conv_buf_shape(d_model, CONV_HALO, tm), F32),
            pltpu.VMEM((tm, d_model), F32),
            pltpu.VMEM((tm, d_model), BF16),
        ],
        compiler_params=pltpu.CompilerParams(
            dimension_semantics=("arbitrary", "arbitrary"), vmem_limit_bytes=VMEM_LIMIT),
        name="mixer_in",
    )(x2, norm_g, w_in, gate_b, cos, sin, dw_w, dw_b, ln_g, ln_b, w_cp, cp_b)


def _retention_body(cdec_ref, q_ref, k_ref, v_ref, sg_ref, inner_ref, cross_ref, state_ref, rg_ref,
                    o_ref, r_scr, *, chunks):
    hd = pl.program_id(1)

    @pl.when(pl.program_id(2) == 0)
    def _():
        r_scr[...] = jnp.zeros_like(r_scr)

    chunk_decay = cdec_ref[hd]
    c_len = RET_CHUNK
    for c in range(chunks):
        rows = slice(c * c_len, (c + 1) * c_len)
        qc = q_ref[rows, :]
        kc = k_ref[rows, :]
        vc = v_ref[rows, :]
        state = r_scr[...]
        s = lax.dot_general(qc, kc, (((1,), (1,)), ((), ())), preferred_element_type=F32) * inner_ref[0]
        inner = jnp.dot(s.astype(BF16), vc, preferred_element_type=F32)
        cross = jnp.dot(qc, state.astype(BF16), preferred_element_type=F32) * cross_ref[0]
        kw = (kc.astype(F32) * state_ref[0]).astype(BF16)
        r_scr[...] = state * chunk_decay + lax.dot_general(
            kw, vc, (((0,), (0,)), ((), ())), preferred_element_type=F32)
        r = inner + cross
        mu = jnp.mean(r, axis=-1, keepdims=True)
        d = r - mu
        var = jnp.mean(d * d, axis=-1, keepdims=True)
        rn = d * lax.rsqrt(var + EPS) * rg_ref[...]
        o_ref[rows, :] = (sg_ref[rows, :].astype(F32) * rn).astype(BF16)


def _retention(q, k, v, sg, ret_norm_g, *, batch, seq, heads, ts=1024):
    tokens = q.shape[0]
    c_len = RET_CHUNK
    s_tiles = seq // ts
    log_gamma = jnp.log1p(-jnp.power(2.0, -5.0 - jnp.arange(heads, dtype=F32)))
    idx = jnp.arange(c_len, dtype=F32)
    diff = idx[:, None] - idx[None, :]
    causal = diff >= 0
    inner_decay = jnp.where(
        causal[None], jnp.exp(jnp.where(causal, diff, 0.0)[None] * log_gamma[:, None, None]), 0.0)
    cross_decay = jnp.exp((idx + 1.0)[None, :] * log_gamma[:, None])[:, :, None]
    state_decay = jnp.exp((c_len - 1.0 - idx)[None, :] * log_gamma[:, None])[:, :, None]
    chunk_decay = jnp.exp(c_len * log_gamma)

    tok = lambda b, h, s: (b * s_tiles + s, h)
    per_head = lambda b, h, s: (h, 0, 0)
    return pl.pallas_call(
        functools.partial(_retention_body, chunks=ts // c_len),
        grid=(batch, heads, s_tiles),
        in_specs=[
            pl.BlockSpec(memory_space=pltpu.SMEM),
            pl.BlockSpec((ts, RET_DK), tok),
            pl.BlockSpec((ts, RET_DK), tok),
            pl.BlockSpec((ts, RET_DV), tok),
            pl.BlockSpec((ts, RET_DV), tok),
            pl.BlockSpec((1, c_len, c_len), per_head),
            pl.BlockSpec((1, c_len, 1), per_head),
            pl.BlockSpec((1, c_len, 1), per_head),
            pl.BlockSpec((1, RET_DV), lambda b, h, s: (0, h)),
        ],
        out_specs=pl.BlockSpec((ts, RET_DV), tok),
        out_shape=jax.ShapeDtypeStruct((tokens, heads * RET_DV), BF16),
        scratch_shapes=[pltpu.VMEM((RET_DK, RET_DV), F32)],
        compiler_params=pltpu.CompilerParams(
            dimension_semantics=("arbitrary", "arbitrary", "arbitrary"), vmem_limit_bytes=VMEM_LIMIT),
        name="retention",
    )(chunk_decay, q, k, v, sg, inner_decay, cross_decay, state_decay, ret_norm_g)


FFN_HALO = SUBLANES


def _post_body(x_ref, r_ref, gya_ref, gtb_ref, wrp_ref, wout_ref, ng_ref, wup_ref, dww_ref, dwb_ref,
               wdn_ref, fg_ref, o_ref, x1_scr, h_scr, carry, ubuf, ff_scr,
               *, tm, ffn_dim, fc, row_block, final_norm):
    @pl.when(pl.program_id(1) == 0)
    def _():
        carry[...] = jnp.zeros_like(carry)

    y_b = jnp.dot(r_ref[...], wrp_ref[...], preferred_element_type=F32)
    mix = (gya_ref[...].astype(F32) + gtb_ref[...].astype(F32) * y_b).astype(BF16)
    x1 = x_ref[...] + jnp.dot(mix, wout_ref[...], preferred_element_type=F32)
    x1_scr[...] = x1
    h_scr[...] = _rms_norm(x1, ng_ref[...]).astype(BF16)

    for j, c in enumerate(range(0, ffn_dim, fc)):
        for half_idx, off in enumerate((c, ffn_dim + c)):
            ub = ubuf.at[j % 2, half_idx]
            ci = 2 * j + half_idx
            ub[:, 0:2 * FFN_HALO, :] = carry[ci]
            _conv_buf_store(ub, FFN_HALO,
                            jnp.dot(h_scr[...], wup_ref[:, off:off + fc], preferred_element_type=F32))
            carry[ci] = ub[:, 2 * tm:2 * (tm + FFN_HALO), :]
        for r0 in range(0, tm, row_block):
            for s in range(fc // LANES):
                u_act, u_lin = (
                    _conv_block(ubuf.at[j % 2, half_idx], FFN_HALO, dww_ref, dwb_ref, r0, row_block, s,
                                off + s * LANES)
                    for half_idx, off in enumerate((c, ffn_dim + c)))
                ff_scr[r0:r0 + row_block, c + s * LANES:c + (s + 1) * LANES] = (
                    _silu(u_act) * u_lin).astype(BF16)

    y = x1_scr[...] + jnp.dot(ff_scr[...], wdn_ref[...], preferred_element_type=F32)
    if final_norm:
        y = _rms_norm(y, fg_ref[...])
    o_ref[...] = y


def _post(x2, r, gya, gtb, w_rp, w_out, norm_g, w_up, dw_w, dw_b, w_down, final_g,
          *, batch, seq, final_norm, tm=512, fc=512):
    tokens, d_model = x2.shape
    ffn_dim = w_down.shape[0]
    s_tiles = seq // tm
    assert dw_w.shape[0] - 1 <= FFN_HALO
    tok = lambda b, s: (b * s_tiles + s, 0)
    return pl.pallas_call(
        functools.partial(_post_body, tm=tm, ffn_dim=ffn_dim, fc=fc, row_block=32, final_norm=final_norm),
        grid=(batch, s_tiles),
        in_specs=[
            pl.BlockSpec((tm, d_model), tok),
            pl.BlockSpec((tm, r.shape[1]), tok),
            pl.BlockSpec((tm, d_model), tok),
            pl.BlockSpec((tm, d_model), tok),
            _resident(w_rp.shape), _resident(w_out.shape),
            _resident(norm_g.shape), _resident(w_up.shape), _resident(dw_w.shape), _resident(dw_b.shape),
            _resident(w_down.shape), _resident(final_g.shape),
        ],
        out_specs=pl.BlockSpec((tm, d_model), tok),
        out_shape=jax.ShapeDtypeStruct((tokens, d_model), F32),
        scratch_shapes=[
            pltpu.VMEM((tm, d_model), F32),
            pltpu.VMEM((tm, d_model), BF16),
            pltpu.VMEM((2 * ffn_dim // fc,) + _conv_buf_shape(fc, FFN_HALO, 0), F32),
            pltpu.VMEM((2, 2) + _conv_buf_shape(fc, FFN_HALO, tm), F32),
            pltpu.VMEM((tm, ffn_dim), BF16),
        ],
        compiler_params=pltpu.CompilerParams(
            dimension_semantics=("arbitrary", "arbitrary"), vmem_limit_bytes=VMEM_LIMIT),
        name="post",
    )(x2, r, gya, gtb, w_rp, w_out, norm_g, w_up, dw_w, dw_b, w_down, final_g)


def kernel(x, norm_mix_g, w_in, gate_b, conv_dw_w, conv_dw_b, conv_ln_g, conv_ln_b, w_conv_proj,
           conv_proj_b, ret_norm_g, w_ret_proj, w_out, norm_ffn_g, w_up, ffn_dw_w, ffn_dw_b, w_down,
           norm_final_g):
    batch, seq, d_model = x.shape
    depth = w_in.shape[0]
    heads = d_model // RET_DK
    half = RET_DK // 2

    pos = jnp.arange(seq, dtype=F32)
    inv_freq = ROPE_BASE ** (-jnp.arange(half, dtype=F32) / half)
    ang = pos[:, None] * inv_freq[None, :]
    cos, sin = jnp.cos(ang), jnp.sin(ang)

    row = lambda v: v.reshape(1, -1)
    x2 = x.reshape(batch * seq, d_model)
    for l in range(depth):
        q, k, v, sg, gya, gtb = _mixer_in(
            x2, row(norm_mix_g[l]), w_in[l].astype(BF16), row(gate_b[l]), cos, sin,
            conv_dw_w[l], row(conv_dw_b[l]), row(conv_ln_g[l]), row(conv_ln_b[l]),
            w_conv_proj[l].astype(BF16), row(conv_proj_b[l]), batch=batch, seq=seq, heads=heads)
        r = _retention(q, k, v, sg, row(ret_norm_g[l]), batch=batch, seq=seq, heads=heads)
        x2 = _post(
            x2, r, gya, gtb, w_ret_proj[l].astype(BF16), w_out[l].astype(BF16),
            row(norm_ffn_g[l]), w_up[l].astype(BF16), ffn_dw_w[l], row(ffn_dw_b[l]),
            w_down[l].astype(BF16), row(norm_final_g), batch=batch, seq=seq,
            final_norm=(l == depth - 1))
    return x2.reshape(batch, seq, d_model)
```

```python
import functools

import jax
import jax.numpy as jnp
from jax import lax
from jax.experimental import pallas as pl
from jax.experimental.pallas import tpu as pltpu

EPS = 1e-6
ROPE_BASE = 10000.0
RET_DK = 256
RET_DV = 2 * RET_DK
RET_CHUNK = 256
LANES = 128
SUBLANES = 8
BF16_ROWS = 2 * SUBLANES
MIB = 1024 * 1024
VMEM_LIMIT = 56 * MIB

BF16 = jnp.bfloat16
F32 = jnp.float32


def _sigmoid(x):
    return 1.0 / (1.0 + jnp.exp(-x))


def _silu(x):
    return x * _sigmoid(x)


def _rms_norm(x, g):
    ms = jnp.mean(x * x, axis=-1, keepdims=True)
    return x * lax.rsqrt(ms + EPS) * g


def _resident(shape):
    nd = len(shape)
    return pl.BlockSpec(shape, lambda *_: (0,) * nd, pipeline_mode=pl.Buffered(1))


def _conv_buf_shape(width, halo, rows):
    return (width // (2 * LANES), 2 * (halo + rows), LANES)


def _conv_buf_store(buf, halo, val, slab0=0):
    rows, width = val.shape
    for i in range(width // LANES):
        s = slab0 + i
        buf[s // 2, pl.ds(2 * halo + s % 2, rows, stride=2), :] = val[:, i * LANES:(i + 1) * LANES]


def _conv_block(buf, halo, w_ref, b_ref, r0, rows, slab, col0):
    taps = w_ref.shape[0]
    base = halo - (taps - 1)
    lanes = slice(col0, col0 + LANES)
    acc = jnp.broadcast_to(b_ref[:, lanes], (rows, LANES))
    for k in range(taps):
        src = buf[slab // 2, pl.ds(2 * (r0 + base + k) + slab % 2, rows, stride=2), :]
        acc = acc + src * w_ref[k:k + 1, lanes]
    return acc


def _conv_halo_shift(buf, halo, rows, first):
    @pl.when(first)
    def _():
        buf[:, 0:2 * halo, :] = jnp.zeros((buf.shape[0], 2 * halo, LANES), F32)

    @pl.when(jnp.logical_not(first))
    def _():
        buf[:, 0:2 * halo, :] = buf[:, 2 * rows:2 * (rows + halo), :]


CONV_HALO = 32
CONV_ROWS = 64


def _mixer_sections(d_model, heads):
    widths = (heads * RET_DK, heads * RET_DK, heads * RET_DV, heads * RET_DV, d_model, d_model)
    edges = [0]
    for w in widths:
        edges.append(edges[-1] + w)
    return tuple(zip(edges[:-1], edges[1:]))


def _mixer_in_body(x_ref, ng_ref, w_ref, gb_ref, cos_ref, sin_ref, dww_ref, dwb_ref, lng_ref, lnb_ref,
                   wcp_ref, cpb_ref, out_ref, h_scr, abuf, cbuf, act_scr, *, tm, d_model, heads):
    q_ref, k_ref, v_ref, sg_ref, gya_ref, gtb_ref = (
        out_ref.at[:, lo:hi] for lo, hi in _mixer_sections(d_model, heads))
    conv_ch = d_model
    off_q = 2 * conv_ch
    off_k = off_q + heads * RET_DK
    off_v = off_k + heads * RET_DK
    off_g = off_v + heads * RET_DV
    off_gt = off_g + heads * RET_DV

    _conv_halo_shift(abuf, CONV_HALO, tm, pl.program_id(1) == 0)
    h_scr[...] = _rms_norm(x_ref[...], ng_ref[...]).astype(BF16)

    def mm(c0, n):
        return jnp.dot(h_scr[...], w_ref[:, c0:c0 + n], preferred_element_type=F32)

    for c in range(0, conv_ch, 256):
        _conv_buf_store(abuf, CONV_HALO, mm(c, 256) * _sigmoid(mm(conv_ch + c, 256)), slab0=c // LANES)

    cos = cos_ref[...]
    sin = sin_ref[...]
    half = RET_DK // 2

    for dst_ref, off, scale in ((q_ref, off_q, RET_DK ** -0.5), (k_ref, off_k, None)):
        for hd in range(heads):
            y = mm(off + hd * RET_DK, RET_DK)
            y1, y2 = y[:, :half], y[:, half:]
            r1 = y1 * cos - y2 * sin
            r2 = y2 * cos + y1 * sin
            if scale is not None:
                r1, r2 = r1 * scale, r2 * scale
            dst_ref[:, hd * RET_DK:hd * RET_DK + half] = r1.astype(BF16)
            dst_ref[:, hd * RET_DK + half:(hd + 1) * RET_DK] = r2.astype(BF16)

    for c in range(0, heads * RET_DV, 512):
        v_ref[:, c:c + 512] = mm(off_v + c, 512).astype(BF16)
    for c in range(0, heads * RET_DV, 512):
        sg_ref[:, c:c + 512] = _silu(mm(off_g + c, 512)).astype(BF16)
    for c in range(0, d_model, 512):
        gtb_ref[:, c:c + 512] = _sigmoid(
            mm(off_gt + d_model + c, 512) + gb_ref[:, d_model + c:d_model + c + 512]).astype(BF16)

    for r0 in range(0, tm, CONV_ROWS):
        for s in range(conv_ch // LANES):
            cbuf[r0:r0 + CONV_ROWS, s * LANES:(s + 1) * LANES] = _conv_block(
                abuf, CONV_HALO, dww_ref, dwb_ref, r0, CONV_ROWS, s, s * LANES)
    for r0 in range(0, tm, BF16_ROWS):
        c = cbuf[r0:r0 + BF16_ROWS, :]
        mu = jnp.mean(c, axis=-1, keepdims=True)
        d = c - mu
        var = jnp.mean(d * d, axis=-1, keepdims=True)
        act_scr[r0:r0 + BF16_ROWS, :] = _silu(
            d * lax.rsqrt(var + EPS) * lng_ref[...] + lnb_ref[...]).astype(BF16)

    for c in range(0, d_model, 512):
        y_a = jnp.dot(act_scr[...], wcp_ref[:, c:c + 512], preferred_element_type=F32) + cpb_ref[:, c:c + 512]
        g_a = _sigmoid(mm(off_gt + c, 512) + gb_ref[:, c:c + 512])
        gya_ref[:, c:c + 512] = (g_a * y_a).astype(BF16)


def _mixer_in(x2, norm_g, w_in, gate_b, cos, sin, dw_w, dw_b, ln_g, ln_b, w_cp, cp_b,
              *, batch, seq, heads, tm=512):
    tokens, d_model = x2.shape
    half = RET_DK // 2
    s_tiles = seq // tm
    assert dw_w.shape[0] - 1 <= CONV_HALO <= tm
    tok = lambda b, s: (b * s_tiles + s, 0)
    pos = lambda b, s: (s, 0)
    out_width = _mixer_sections(d_model, heads)[-1][1]
    return pl.pallas_call(
        functools.partial(_mixer_in_body, tm=tm, d_model=d_model, heads=heads),
        grid=(batch, s_tiles),
        in_specs=[
            pl.BlockSpec((tm, d_model), tok),
            _resident(norm_g.shape), _resident(w_in.shape), _resident(gate_b.shape),
            pl.BlockSpec((tm, half), pos), pl.BlockSpec((tm, half), pos),
            _resident(dw_w.shape), _resident(dw_b.shape), _resident(ln_g.shape), _resident(ln_b.shape),
            _resident(w_cp.shape), _resident(cp_b.shape),
        ],
        out_specs=pl.BlockSpec((tm, out_width), tok),
        out_shape=jax.ShapeDtypeStruct((tokens, out_width), BF16),
        scratch_shapes=[
            pltpu.VMEM((tm, d_model), BF16),
            pltpu.VMEM(_conv_buf_shape(d_model, CONV_HALO, tm), F32),
            pltpu.VMEM((tm, d_model), F32),
            pltpu.VMEM((tm, d_model), BF16),
        ],
        compiler_params=pltpu.CompilerParams(
            dimension_semantics=("arbitrary", "arbitrary"), vmem_limit_bytes=VMEM_LIMIT),
        name="mixer_in",
    )(x2, norm_g, w_in, gate_b, cos, sin, dw_w, dw_b, ln_g, ln_b, w_cp, cp_b)


def _retention_body(cdec_ref, q_ref, k_ref, v_ref, sg_ref, inner_ref, cross_ref, state_ref, rg_ref,
                    o_ref, r_scr, *, chunks, unroll):
    r_scr[...] = jnp.zeros_like(r_scr)
    chunk_decay = cdec_ref[pl.program_id(1)]
    c_len = RET_CHUNK

    def chunk_group(i, carry):
        base = pl.multiple_of(i * (unroll * c_len), unroll * c_len)
        for c in range(unroll):
            rows = pl.ds(base + c * c_len, c_len)
            qc = q_ref[rows, :]
            kc = k_ref[rows, :]
            vc = v_ref[rows, :]
            state = r_scr[...]
            s = lax.dot_general(qc, kc, (((1,), (1,)), ((), ())), preferred_element_type=F32) * inner_ref[0]
            inner = jnp.dot(s.astype(BF16), vc, preferred_element_type=F32)
            cross = jnp.dot(qc, state.astype(BF16), preferred_element_type=F32) * cross_ref[0]
            kw = (kc.astype(F32) * state_ref[0]).astype(BF16)
            r_scr[...] = state * chunk_decay + lax.dot_general(
                kw, vc, (((0,), (0,)), ((), ())), preferred_element_type=F32)
            r = inner + cross
            mu = jnp.mean(r, axis=-1, keepdims=True)
            d = r - mu
            var = jnp.mean(d * d, axis=-1, keepdims=True)
            rn = d * lax.rsqrt(var + EPS) * rg_ref[...]
            o_ref[rows, :] = (sg_ref[rows, :].astype(F32) * rn).astype(BF16)
        return carry

    lax.fori_loop(0, chunks // unroll, chunk_group, 0)


def _retention(mixed, ret_norm_g, *, batch, seq, heads, d_model, unroll=4):
    tokens = mixed.shape[0]
    c_len = RET_CHUNK
    assert seq % (unroll * c_len) == 0
    (q0, _), (k0, _), (v0, _), (g0, _) = _mixer_sections(d_model, heads)[:4]
    assert q0 % RET_DK == k0 % RET_DK == v0 % RET_DV == g0 % RET_DV == 0
    head_block = lambda col0, width: (lambda b, h: (b, col0 // width + h))
    log_gamma = jnp.log1p(-jnp.power(2.0, -5.0 - jnp.arange(heads, dtype=F32)))
    idx = jnp.arange(c_len, dtype=F32)
    diff = idx[:, None] - idx[None, :]
    causal = diff >= 0
    inner_decay = jnp.where(
        causal[None], jnp.exp(jnp.where(causal, diff, 0.0)[None] * log_gamma[:, None, None]), 0.0)
    cross_decay = jnp.exp((idx + 1.0)[None, :] * log_gamma[:, None])[:, :, None]
    state_decay = jnp.exp((c_len - 1.0 - idx)[None, :] * log_gamma[:, None])[:, :, None]
    chunk_decay = jnp.exp(c_len * log_gamma)

    seq_head = lambda b, h: (b, h)
    per_head = lambda b, h: (h, 0, 0)
    return pl.pallas_call(
        functools.partial(_retention_body, chunks=seq // c_len, unroll=unroll),
        grid=(batch, heads),
        in_specs=[
            pl.BlockSpec(memory_space=pltpu.SMEM),
            pl.BlockSpec((seq, RET_DK), head_block(q0, RET_DK)),
            pl.BlockSpec((seq, RET_DK), head_block(k0, RET_DK)),
            pl.BlockSpec((seq, RET_DV), head_block(v0, RET_DV)),
            pl.BlockSpec((seq, RET_DV), head_block(g0, RET_DV)),
            pl.BlockSpec((1, c_len, c_len), per_head),
            pl.BlockSpec((1, c_len, 1), per_head),
            pl.BlockSpec((1, c_len, 1), per_head),
            pl.BlockSpec((1, RET_DV), lambda b, h: (0, h)),
        ],
        out_specs=pl.BlockSpec((seq, RET_DV), seq_head),
        out_shape=jax.ShapeDtypeStruct((tokens, heads * RET_DV), BF16),
        scratch_shapes=[pltpu.VMEM((RET_DK, RET_DV), F32)],
        compiler_params=pltpu.CompilerParams(
            dimension_semantics=("arbitrary", "arbitrary"), vmem_limit_bytes=VMEM_LIMIT),
        name="retention",
    )(chunk_decay, mixed, mixed, mixed, mixed, inner_decay, cross_decay, state_decay, ret_norm_g)


FFN_HALO = SUBLANES


def _post_body(x_ref, r_ref, gya_ref, gtb_ref, wrp_ref, wout_ref, ng_ref, wup_ref, dww_ref, dwb_ref,
               wdn_ref, fg_ref, o_ref, x1_scr, h_scr, carry, ubuf, ff_scr,
               *, tm, ffn_dim, fc, row_block, final_norm):
    @pl.when(pl.program_id(1) == 0)
    def _():
        carry[...] = jnp.zeros_like(carry)

    y_b = jnp.dot(r_ref[...], wrp_ref[...], preferred_element_type=F32)
    mix = (gya_ref[...].astype(F32) + gtb_ref[...].astype(F32) * y_b).astype(BF16)
    x1 = x_ref[...] + jnp.dot(mix, wout_ref[...], preferred_element_type=F32)
    x1_scr[...] = x1
    h_scr[...] = _rms_norm(x1, ng_ref[...]).astype(BF16)

    for j, c in enumerate(range(0, ffn_dim, fc)):
        for half_idx, off in enumerate((c, ffn_dim + c)):
            ub = ubuf.at[j % 2, half_idx]
            ci = 2 * j + half_idx
            ub[:, 0:2 * FFN_HALO, :] = carry[ci]
            _conv_buf_store(ub, FFN_HALO,
                            jnp.dot(h_scr[...], wup_ref[:, off:off + fc], preferred_element_type=F32))
            carry[ci] = ub[:, 2 * tm:2 * (tm + FFN_HALO), :]
        for r0 in range(0, tm, row_block):
            for s in range(fc // LANES):
                u_act, u_lin = (
                    _conv_block(ubuf.at[j % 2, half_idx], FFN_HALO, dww_ref, dwb_ref, r0, row_block, s,
                                off + s * LANES)
                    for half_idx, off in enumerate((c, ffn_dim + c)))
                ff_scr[r0:r0 + row_block, c + s * LANES:c + (s + 1) * LANES] = (
                    _silu(u_act) * u_lin).astype(BF16)

    y = x1_scr[...] + jnp.dot(ff_scr[...], wdn_ref[...], preferred_element_type=F32)
    if final_norm:
        y = _rms_norm(y, fg_ref[...])
    o_ref[...] = y


def _post(x2, r, mixed, w_rp, w_out, norm_g, w_up, dw_w, dw_b, w_down, final_g,
          *, batch, seq, heads, final_norm, tm=512, fc=512):
    tokens, d_model = x2.shape
    ffn_dim = w_down.shape[0]
    s_tiles = seq // tm
    assert dw_w.shape[0] - 1 <= FFN_HALO
    tok = lambda b, s: (b * s_tiles + s, 0)
    (gya0, _), (gtb0, _) = _mixer_sections(d_model, heads)[4:]
    assert gya0 % d_model == gtb0 % d_model == 0
    tok_section = lambda col0: (lambda b, s: (b * s_tiles + s, col0 // d_model))
    return pl.pallas_call(
        functools.partial(_post_body, tm=tm, ffn_dim=ffn_dim, fc=fc, row_block=32, final_norm=final_norm),
        grid=(batch, s_tiles),
        in_specs=[
            pl.BlockSpec((tm, d_model), tok),
            pl.BlockSpec((tm, r.shape[1]), tok),
            pl.BlockSpec((tm, d_model), tok_section(gya0)),
            pl.BlockSpec((tm, d_model), tok_section(gtb0)),
            _resident(w_rp.shape), _resident(w_out.shape),
            _resident(norm_g.shape), _resident(w_up.shape), _resident(dw_w.shape), _resident(dw_b.shape),
            _resident(w_down.shape), _resident(final_g.shape),
        ],
        out_specs=pl.BlockSpec((tm, d_model), tok),
        out_shape=jax.ShapeDtypeStruct((tokens, d_model), F32),
        scratch_shapes=[
            pltpu.VMEM((tm, d_model), F32),
            pltpu.VMEM((tm, d_model), BF16),
            pltpu.VMEM((2 * ffn_dim // fc,) + _conv_buf_shape(fc, FFN_HALO, 0), F32),
            pltpu.VMEM((2, 2) + _conv_buf_shape(fc, FFN_HALO, tm), F32),
            pltpu.VMEM((tm, ffn_dim), BF16),
        ],
        compiler_params=pltpu.CompilerParams(
            dimension_semantics=("arbitrary", "arbitrary"), vmem_limit_bytes=VMEM_LIMIT),
        name="post",
    )(x2, r, mixed, mixed, w_rp, w_out, norm_g, w_up, dw_w, dw_b, w_down, final_g)


def kernel(x, norm_mix_g, w_in, gate_b, conv_dw_w, conv_dw_b, conv_ln_g, conv_ln_b, w_conv_proj,
           conv_proj_b, ret_norm_g, w_ret_proj, w_out, norm_ffn_g, w_up, ffn_dw_w, ffn_dw_b, w_down,
           norm_final_g):
    batch, seq, d_model = x.shape
    depth = w_in.shape[0]
    heads = d_model // RET_DK
    half = RET_DK // 2

    pos = jnp.arange(seq, dtype=F32)
    inv_freq = ROPE_BASE ** (-jnp.arange(half, dtype=F32) / half)
    ang = pos[:, None] * inv_freq[None, :]
    cos, sin = jnp.cos(ang), jnp.sin(ang)

    row = lambda v: v.reshape(1, -1)
    x2 = x.reshape(batch * seq, d_model)
    for l in range(depth):
        mixed = _mixer_in(
            x2, row(norm_mix_g[l]), w_in[l].astype(BF16), row(gate_b[l]), cos, sin,
            conv_dw_w[l], row(conv_dw_b[l]), row(conv_ln_g[l]), row(conv_ln_b[l]),
            w_conv_proj[l].astype(BF16), row(conv_proj_b[l]), batch=batch, seq=seq, heads=heads)
        r = _retention(mixed, row(ret_norm_g[l]), batch=batch, seq=seq, heads=heads, d_model=d_model)
        x2 = _post(
            x2, r, mixed, w_ret_proj[l].astype(BF16), w_out[l].astype(BF16),
            row(norm_ffn_g[l]), w_up[l].astype(BF16), ffn_dw_w[l], row(ffn_dw_b[l]),
            w_down[l].astype(BF16), row(norm_final_g), batch=batch, seq=seq, heads=heads,
            final_norm=(l == depth - 1))
    return x2.reshape(batch, seq, d_model)
```

```python
import functools

import jax
import jax.numpy as jnp
from jax import lax
from jax.experimental import pallas as pl
from jax.experimental.pallas import tpu as pltpu

EPS = 1e-6
ROPE_BASE = 10000.0
RET_DK = 256
RET_DV = 2 * RET_DK
RET_CHUNK = 256
LANES = 128
SUBLANES = 8
BF16_ROWS = 2 * SUBLANES
MIB = 1024 * 1024
VMEM_LIMIT = 60 * MIB

BF16 = jnp.bfloat16
F32 = jnp.float32


def _sigmoid(x):
    return 1.0 / (1.0 + jnp.exp(-x))


def _silu(x):
    return x * _sigmoid(x)


def _rms_norm(x, g):
    ms = jnp.mean(x * x, axis=-1, keepdims=True)
    return x * lax.rsqrt(ms + EPS) * g


def _resident(shape):
    nd = len(shape)
    return pl.BlockSpec(shape, lambda *_: (0,) * nd, pipeline_mode=pl.Buffered(1))


def _conv_buf_shape(width, halo, rows):
    return (width // (2 * LANES), 2 * (halo + rows), LANES)


def _conv_buf_store(buf, halo, val, slab0=0):
    rows, width = val.shape
    for i in range(width // LANES):
        s = slab0 + i
        buf[s // 2, pl.ds(2 * halo + s % 2, rows, stride=2), :] = val[:, i * LANES:(i + 1) * LANES]


def _conv_block(buf, halo, w_ref, b_ref, r0, rows, slab, col0):
    taps = w_ref.shape[0]
    base = halo - (taps - 1)
    lanes = slice(col0, col0 + LANES)
    acc = jnp.broadcast_to(b_ref[:, lanes], (rows, LANES))
    for k in range(taps):
        src = buf[slab // 2, pl.ds(2 * (r0 + base + k) + slab % 2, rows, stride=2), :]
        acc = acc + src * w_ref[k:k + 1, lanes]
    return acc


def _conv_halo_shift(buf, halo, rows, first):
    @pl.when(first)
    def _():
        buf[:, 0:2 * halo, :] = jnp.zeros((buf.shape[0], 2 * halo, LANES), F32)

    @pl.when(jnp.logical_not(first))
    def _():
        buf[:, 0:2 * halo, :] = buf[:, 2 * rows:2 * (rows + halo), :]


CONV_HALO = 32
CONV_ROWS = 64


def _mixer_sections(d_model, heads):
    widths = (heads * RET_DK, heads * RET_DK, heads * RET_DV, heads * RET_DV, d_model, d_model)
    edges = [0]
    for w in widths:
        edges.append(edges[-1] + w)
    return tuple(zip(edges[:-1], edges[1:]))


def _mixer_in_body(x_ref, ng_ref, w_ref, gb_ref, cos_ref, sin_ref, dww_ref, dwb_ref, lng_ref, lnb_ref,
                   wcp_ref, cpb_ref, out_ref, h_scr, abuf, cbuf, act_scr, *, tm, d_model, heads):
    q_ref, k_ref, v_ref, sg_ref, gya_ref, gtb_ref = (
        out_ref.at[:, lo:hi] for lo, hi in _mixer_sections(d_model, heads))
    conv_ch = d_model
    off_q = 2 * conv_ch
    off_k = off_q + heads * RET_DK
    off_v = off_k + heads * RET_DK
    off_g = off_v + heads * RET_DV
    off_gt = off_g + heads * RET_DV

    _conv_halo_shift(abuf, CONV_HALO, tm, pl.program_id(1) == 0)
    h_scr[...] = _rms_norm(x_ref[...], ng_ref[...]).astype(BF16)

    def mm(c0, n):
        return jnp.dot(h_scr[...], w_ref[:, c0:c0 + n], preferred_element_type=F32)

    for c in range(0, conv_ch, 256):
        _conv_buf_store(abuf, CONV_HALO, mm(c, 256) * _sigmoid(mm(conv_ch + c, 256)), slab0=c // LANES)

    cos = cos_ref[...]
    sin = sin_ref[...]
    half = RET_DK // 2

    for dst_ref, off, scale in ((q_ref, off_q, RET_DK ** -0.5), (k_ref, off_k, None)):
        for hd in range(heads):
            y = mm(off + hd * RET_DK, RET_DK)
            y1, y2 = y[:, :half], y[:, half:]
            r1 = y1 * cos - y2 * sin
            r2 = y2 * cos + y1 * sin
            if scale is not None:
                r1, r2 = r1 * scale, r2 * scale
            dst_ref[:, hd * RET_DK:hd * RET_DK + half] = r1.astype(BF16)
            dst_ref[:, hd * RET_DK + half:(hd + 1) * RET_DK] = r2.astype(BF16)

    for c in range(0, heads * RET_DV, 512):
        v_ref[:, c:c + 512] = mm(off_v + c, 512).astype(BF16)
    for c in range(0, heads * RET_DV, 512):
        sg_ref[:, c:c + 512] = _silu(mm(off_g + c, 512)).astype(BF16)
    for c in range(0, d_model, 512):
        gtb_ref[:, c:c + 512] = _sigmoid(
            mm(off_gt + d_model + c, 512) + gb_ref[:, d_model + c:d_model + c + 512]).astype(BF16)

    for r0 in range(0, tm, CONV_ROWS):
        for s in range(conv_ch // LANES):
            cbuf[r0:r0 + CONV_ROWS, s * LANES:(s + 1) * LANES] = _conv_block(
                abuf, CONV_HALO, dww_ref, dwb_ref, r0, CONV_ROWS, s, s * LANES)
    for r0 in range(0, tm, BF16_ROWS):
        c = cbuf[r0:r0 + BF16_ROWS, :]
        mu = jnp.mean(c, axis=-1, keepdims=True)
        d = c - mu
        var = jnp.mean(d * d, axis=-1, keepdims=True)
        act_scr[r0:r0 + BF16_ROWS, :] = _silu(
            d * lax.rsqrt(var + EPS) * lng_ref[...] + lnb_ref[...]).astype(BF16)

    for c in range(0, d_model, 512):
        y_a = jnp.dot(act_scr[...], wcp_ref[:, c:c + 512], preferred_element_type=F32) + cpb_ref[:, c:c + 512]
        g_a = _sigmoid(mm(off_gt + c, 512) + gb_ref[:, c:c + 512])
        gya_ref[:, c:c + 512] = (g_a * y_a).astype(BF16)


def _mixer_in(x2, norm_g, w_in, gate_b, cos, sin, dw_w, dw_b, ln_g, ln_b, w_cp, cp_b,
              *, batch, seq, heads, tm=512):
    tokens, d_model = x2.shape
    half = RET_DK // 2
    s_tiles = seq // tm
    assert dw_w.shape[0] - 1 <= CONV_HALO <= tm
    tok = lambda b, s: (b * s_tiles + s, 0)
    pos = lambda b, s: (s, 0)
    out_width = _mixer_sections(d_model, heads)[-1][1]
    return pl.pallas_call(
        functools.partial(_mixer_in_body, tm=tm, d_model=d_model, heads=heads),
        grid=(batch, s_tiles),
        in_specs=[
            pl.BlockSpec((tm, d_model), tok),
            _resident(norm_g.shape), _resident(w_in.shape), _resident(gate_b.shape),
            pl.BlockSpec((tm, half), pos), pl.BlockSpec((tm, half), pos),
            _resident(dw_w.shape), _resident(dw_b.shape), _resident(ln_g.shape), _resident(ln_b.shape),
            _resident(w_cp.shape), _resident(cp_b.shape),
        ],
        out_specs=pl.BlockSpec((tm, out_width), tok),
        out_shape=jax.ShapeDtypeStruct((tokens, out_width), BF16),
        scratch_shapes=[
            pltpu.VMEM((tm, d_model), BF16),
            pltpu.VMEM(_conv_buf_shape(d_model, CONV_HALO, tm), F32),
            pltpu.VMEM((tm, d_model), F32),
            pltpu.VMEM((tm, d_model), BF16),
        ],
        compiler_params=pltpu.CompilerParams(
            dimension_semantics=("arbitrary", "arbitrary"), vmem_limit_bytes=VMEM_LIMIT),
        name="mixer_in",
    )(x2, norm_g, w_in, gate_b, cos, sin, dw_w, dw_b, ln_g, ln_b, w_cp, cp_b)


def _retention_body(cdec_ref, q_ref, k_ref, v_ref, sg_ref, inner_ref, cross_ref, state_ref, rg_ref,
                    o_ref, r_scr, *, chunks, unroll):
    r_scr[...] = jnp.zeros_like(r_scr)
    chunk_decay = cdec_ref[pl.program_id(1)]
    c_len = RET_CHUNK

    def chunk_group(i, carry):
        base = pl.multiple_of(i * (unroll * c_len), unroll * c_len)
        for c in range(unroll):
            rows = pl.ds(base + c * c_len, c_len)
            qc = q_ref[rows, :]
            kc = k_ref[rows, :]
            vc = v_ref[rows, :]
            state = r_scr[...]
            s = lax.dot_general(qc, kc, (((1,), (1,)), ((), ())), preferred_element_type=F32) * inner_ref[0]
            inner = jnp.dot(s.astype(BF16), vc, preferred_element_type=F32)
            cross = jnp.dot(qc, state.astype(BF16), preferred_element_type=F32) * cross_ref[0]
            kw = (kc.astype(F32) * state_ref[0]).astype(BF16)
            r_scr[...] = state * chunk_decay + lax.dot_general(
                kw, vc, (((0,), (0,)), ((), ())), preferred_element_type=F32)
            r = inner + cross
            mu = jnp.mean(r, axis=-1, keepdims=True)
            d = r - mu
            var = jnp.mean(d * d, axis=-1, keepdims=True)
            rn = d * lax.rsqrt(var + EPS) * rg_ref[...]
            o_ref[rows, :] = (sg_ref[rows, :].astype(F32) * rn).astype(BF16)
        return carry

    lax.fori_loop(0, chunks // unroll, chunk_group, 0)


def _retention(mixed, ret_norm_g, *, batch, seq, heads, d_model, unroll=4):
    tokens = mixed.shape[0]
    c_len = RET_CHUNK
    assert seq % (unroll * c_len) == 0
    (q0, _), (k0, _), (v0, _), (g0, _) = _mixer_sections(d_model, heads)[:4]
    assert q0 % RET_DK == k0 % RET_DK == v0 % RET_DV == g0 % RET_DV == 0
    head_block = lambda col0, width: (lambda b, h: (b, col0 // width + h))
    log_gamma = jnp.log1p(-jnp.power(2.0, -5.0 - jnp.arange(heads, dtype=F32)))
    idx = jnp.arange(c_len, dtype=F32)
    diff = idx[:, None] - idx[None, :]
    causal = diff >= 0
    inner_decay = jnp.where(
        causal[None], jnp.exp(jnp.where(causal, diff, 0.0)[None] * log_gamma[:, None, None]), 0.0)
    cross_decay = jnp.exp((idx + 1.0)[None, :] * log_gamma[:, None])[:, :, None]
    state_decay = jnp.exp((c_len - 1.0 - idx)[None, :] * log_gamma[:, None])[:, :, None]
    chunk_decay = jnp.exp(c_len * log_gamma)

    seq_head = lambda b, h: (b, h)
    per_head = lambda b, h: (h, 0, 0)
    return pl.pallas_call(
        functools.partial(_retention_body, chunks=seq // c_len, unroll=unroll),
        grid=(batch, heads),
        in_specs=[
            pl.BlockSpec(memory_space=pltpu.SMEM),
            pl.BlockSpec((seq, RET_DK), head_block(q0, RET_DK)),
            pl.BlockSpec((seq, RET_DK), head_block(k0, RET_DK)),
            pl.BlockSpec((seq, RET_DV), head_block(v0, RET_DV)),
            pl.BlockSpec((seq, RET_DV), head_block(g0, RET_DV)),
            pl.BlockSpec((1, c_len, c_len), per_head),
            pl.BlockSpec((1, c_len, 1), per_head),
            pl.BlockSpec((1, c_len, 1), per_head),
            pl.BlockSpec((1, RET_DV), lambda b, h: (0, h)),
        ],
        out_specs=pl.BlockSpec((seq, RET_DV), seq_head),
        out_shape=jax.ShapeDtypeStruct((tokens, heads * RET_DV), BF16),
        scratch_shapes=[pltpu.VMEM((RET_DK, RET_DV), F32)],
        compiler_params=pltpu.CompilerParams(
            dimension_semantics=("arbitrary", "arbitrary"), vmem_limit_bytes=VMEM_LIMIT),
        name="retention",
    )(chunk_decay, mixed, mixed, mixed, mixed, inner_decay, cross_decay, state_decay, ret_norm_g)


FFN_HALO = SUBLANES


def _post_body(x_ref, r_ref, gya_ref, gtb_ref, wrp_ref, wout_ref, ng_ref, wup_ref, dww_ref, dwb_ref,
               wdn_ref, fg_ref, o_ref, x1_scr, h_scr, carry, ubuf, ff_scr,
               *, tm, ffn_dim, fc, row_block, final_norm):
    @pl.when(pl.program_id(1) == 0)
    def _():
        carry[...] = jnp.zeros_like(carry)

    y_b = jnp.dot(r_ref[...], wrp_ref[...], preferred_element_type=F32)
    mix = (gya_ref[...].astype(F32) + gtb_ref[...].astype(F32) * y_b).astype(BF16)
    x1 = x_ref[...] + jnp.dot(mix, wout_ref[...], preferred_element_type=F32)
    x1_scr[...] = x1
    h_scr[...] = _rms_norm(x1, ng_ref[...]).astype(BF16)

    for j, c in enumerate(range(0, ffn_dim, fc)):
        for half_idx, off in enumerate((c, ffn_dim + c)):
            ub = ubuf.at[j % 2, half_idx]
            ci = 2 * j + half_idx
            ub[:, 0:2 * FFN_HALO, :] = carry[ci]
            _conv_buf_store(ub, FFN_HALO,
                            jnp.dot(h_scr[...], wup_ref[:, off:off + fc], preferred_element_type=F32))
            carry[ci] = ub[:, 2 * tm:2 * (tm + FFN_HALO), :]
        for r0 in range(0, tm, row_block):
            for s in range(fc // LANES):
                u_act, u_lin = (
                    _conv_block(ubuf.at[j % 2, half_idx], FFN_HALO, dww_ref, dwb_ref, r0, row_block, s,
                                off + s * LANES)
                    for half_idx, off in enumerate((c, ffn_dim + c)))
                ff_scr[r0:r0 + row_block, c + s * LANES:c + (s + 1) * LANES] = (
                    _silu(u_act) * u_lin).astype(BF16)

    y = x1_scr[...] + jnp.dot(ff_scr[...], wdn_ref[...], preferred_element_type=F32)
    if final_norm:
        y = _rms_norm(y, fg_ref[...])
    o_ref[...] = y


def _post(x2, r, mixed, w_rp, w_out, norm_g, w_up, dw_w, dw_b, w_down, final_g,
          *, batch, seq, heads, final_norm, tm=512, fc=1024):
    tokens, d_model = x2.shape
    ffn_dim = w_down.shape[0]
    s_tiles = seq // tm
    assert dw_w.shape[0] - 1 <= FFN_HALO
    tok = lambda b, s: (b * s_tiles + s, 0)
    (gya0, _), (gtb0, _) = _mixer_sections(d_model, heads)[4:]
    assert gya0 % d_model == gtb0 % d_model == 0
    tok_section = lambda col0: (lambda b, s: (b * s_tiles + s, col0 // d_model))
    return pl.pallas_call(
        functools.partial(_post_body, tm=tm, ffn_dim=ffn_dim, fc=fc, row_block=32, final_norm=final_norm),
        grid=(batch, s_tiles),
        in_specs=[
            pl.BlockSpec((tm, d_model), tok),
            pl.BlockSpec((tm, r.shape[1]), tok),
            pl.BlockSpec((tm, d_model), tok_section(gya0)),
            pl.BlockSpec((tm, d_model), tok_section(gtb0)),
            _resident(w_rp.shape), _resident(w_out.shape),
            _resident(norm_g.shape), _resident(w_up.shape), _resident(dw_w.shape), _resident(dw_b.shape),
            _resident(w_down.shape), _resident(final_g.shape),
        ],
        out_specs=pl.BlockSpec((tm, d_model), tok),
        out_shape=jax.ShapeDtypeStruct((tokens, d_model), F32),
        scratch_shapes=[
            pltpu.VMEM((tm, d_model), F32),
            pltpu.VMEM((tm, d_model), BF16),
            pltpu.VMEM((2 * ffn_dim // fc,) + _conv_buf_shape(fc, FFN_HALO, 0), F32),
            pltpu.VMEM((2, 2) + _conv_buf_shape(fc, FFN_HALO, tm), F32),
            pltpu.VMEM((tm, ffn_dim), BF16),
        ],
        compiler_params=pltpu.CompilerParams(
            dimension_semantics=("arbitrary", "arbitrary"), vmem_limit_bytes=VMEM_LIMIT),
        name="post",
    )(x2, r, mixed, mixed, w_rp, w_out, norm_g, w_up, dw_w, dw_b, w_down, final_g)


def kernel(x, norm_mix_g, w_in, gate_b, conv_dw_w, conv_dw_b, conv_ln_g, conv_ln_b, w_conv_proj,
           conv_proj_b, ret_norm_g, w_ret_proj, w_out, norm_ffn_g, w_up, ffn_dw_w, ffn_dw_b, w_down,
           norm_final_g):
    batch, seq, d_model = x.shape
    depth = w_in.shape[0]
    heads = d_model // RET_DK
    half = RET_DK // 2

    pos = jnp.arange(seq, dtype=F32)
    inv_freq = ROPE_BASE ** (-jnp.arange(half, dtype=F32) / half)
    ang = pos[:, None] * inv_freq[None, :]
    cos, sin = jnp.cos(ang), jnp.sin(ang)

    row = lambda v: v.reshape(1, -1)
    x2 = x.reshape(batch * seq, d_model)
    for l in range(depth):
        mixed = _mixer_in(
            x2, row(norm_mix_g[l]), w_in[l].astype(BF16), row(gate_b[l]), cos, sin,
            conv_dw_w[l], row(conv_dw_b[l]), row(conv_ln_g[l]), row(conv_ln_b[l]),
            w_conv_proj[l].astype(BF16), row(conv_proj_b[l]), batch=batch, seq=seq, heads=heads)
        r = _retention(mixed, row(ret_norm_g[l]), batch=batch, seq=seq, heads=heads, d_model=d_model)
        x2 = _post(
            x2, r, mixed, w_ret_proj[l].astype(BF16), w_out[l].astype(BF16),
            row(norm_ffn_g[l]), w_up[l].astype(BF16), ffn_dw_w[l], row(ffn_dw_b[l]),
            w_down[l].astype(BF16), row(norm_final_g), batch=batch, seq=seq, heads=heads,
            final_norm=(l == depth - 1))
    return x2.reshape(batch, seq, d_model)
```

```python
import functools

import jax
import jax.numpy as jnp
from jax import lax
from jax.experimental import pallas as pl
from jax.experimental.pallas import tpu as pltpu

EPS = 1e-6
ROPE_BASE = 10000.0
RET_DK = 256
RET_DV = 2 * RET_DK
RET_CHUNK = 256
LANES = 128
MXU_COLS = 256
PROJ_COLS = 2 * MXU_COLS
SUBLANES = 8
BF16_ROWS = 2 * SUBLANES
MIB = 1024 * 1024
VMEM_LIMIT = 56 * MIB

BF16 = jnp.bfloat16
F32 = jnp.float32


def _sigmoid(x):
    return 1.0 / (1.0 + jnp.exp(-x))


def _silu(x):
    return x * _sigmoid(x)


def _rms_norm(x, g):
    ms = jnp.mean(x * x, axis=-1, keepdims=True)
    return x * lax.rsqrt(ms + EPS) * g


def _resident(shape):
    nd = len(shape)
    return pl.BlockSpec(shape, lambda *_: (0,) * nd, pipeline_mode=pl.Buffered(1))


def _conv_buf_shape(width, halo, rows):
    return (width // (2 * LANES), 2 * (halo + rows), LANES)


def _conv_buf_store(buf, halo, val, slab0=0):
    rows, width = val.shape
    for i in range(width // LANES):
        s = slab0 + i
        buf[s // 2, pl.ds(2 * halo + s % 2, rows, stride=2), :] = val[:, i * LANES:(i + 1) * LANES]


def _conv_block(buf, halo, w_ref, b_ref, r0, rows, slab, col0):
    taps = w_ref.shape[0]
    base = halo - (taps - 1)
    lanes = slice(col0, col0 + LANES)
    acc = jnp.broadcast_to(b_ref[:, lanes], (rows, LANES))
    for k in range(taps):
        src = buf[slab // 2, pl.ds(2 * (r0 + base + k) + slab % 2, rows, stride=2), :]
        acc = acc + src * w_ref[k:k + 1, lanes]
    return acc


def _conv_halo_shift(buf, halo, rows, first):
    @pl.when(first)
    def _():
        buf[:, 0:2 * halo, :] = jnp.zeros((buf.shape[0], 2 * halo, LANES), F32)

    @pl.when(jnp.logical_not(first))
    def _():
        buf[:, 0:2 * halo, :] = buf[:, 2 * rows:2 * (rows + halo), :]


CONV_HALO = 32


def _mixer_sections(d_model, heads):
    widths = (heads * RET_DK, heads * RET_DK, heads * RET_DV, heads * RET_DV, d_model, d_model)
    edges = [0]
    for w in widths:
        edges.append(edges[-1] + w)
    return tuple(zip(edges[:-1], edges[1:]))


def _mixer_in_body(x_ref, ng_ref, w_ref, gb_ref, cos_ref, sin_ref, dww_ref, dwb_ref, lng_ref, lnb_ref,
                   wcp_ref, cpb_ref, out_ref, h_scr, abuf, act_scr, *, tm, d_model, heads):
    q_ref, k_ref, v_ref, sg_ref, gya_ref, gtb_ref = (
        out_ref.at[:, lo:hi] for lo, hi in _mixer_sections(d_model, heads))
    conv_ch = d_model
    off_q = 2 * conv_ch
    off_k = off_q + heads * RET_DK
    off_v = off_k + heads * RET_DK
    off_g = off_v + heads * RET_DV
    off_gt = off_g + heads * RET_DV

    _conv_halo_shift(abuf, CONV_HALO, tm, pl.program_id(1) == 0)
    h_scr[...] = _rms_norm(x_ref[...], ng_ref[...]).astype(BF16)

    def mm(c0, n):
        return jnp.dot(h_scr[...], w_ref[:, c0:c0 + n], preferred_element_type=F32)

    for c in range(0, conv_ch, MXU_COLS):
        _conv_buf_store(abuf, CONV_HALO, mm(c, MXU_COLS) * _sigmoid(mm(conv_ch + c, MXU_COLS)),
                        slab0=c // LANES)

    cos = cos_ref[...]
    sin = sin_ref[...]
    half = RET_DK // 2

    for dst_ref, off, scale in ((q_ref, off_q, RET_DK ** -0.5), (k_ref, off_k, None)):
        for hd in range(heads):
            y = mm(off + hd * RET_DK, RET_DK)
            y1, y2 = y[:, :half], y[:, half:]
            r1 = y1 * cos - y2 * sin
            r2 = y2 * cos + y1 * sin
            if scale is not None:
                r1, r2 = r1 * scale, r2 * scale
            dst_ref[:, hd * RET_DK:hd * RET_DK + half] = r1.astype(BF16)
            dst_ref[:, hd * RET_DK + half:(hd + 1) * RET_DK] = r2.astype(BF16)

    n = PROJ_COLS
    for c in range(0, heads * RET_DV, n):
        v_ref[:, c:c + n] = mm(off_v + c, n).astype(BF16)
    for c in range(0, heads * RET_DV, n):
        sg_ref[:, c:c + n] = _silu(mm(off_g + c, n)).astype(BF16)
    for c in range(0, d_model, n):
        gtb_ref[:, c:c + n] = _sigmoid(
            mm(off_gt + d_model + c, n) + gb_ref[:, d_model + c:d_model + c + n]).astype(BF16)

    for r0 in range(0, tm, BF16_ROWS):
        c = jnp.concatenate(
            [_conv_block(abuf, CONV_HALO, dww_ref, dwb_ref, r0, BF16_ROWS, s, s * LANES)
             for s in range(conv_ch // LANES)], axis=-1)
        mu = jnp.mean(c, axis=-1, keepdims=True)
        d = c - mu
        var = jnp.mean(d * d, axis=-1, keepdims=True)
        act_scr[r0:r0 + BF16_ROWS, :] = _silu(
            d * lax.rsqrt(var + EPS) * lng_ref[...] + lnb_ref[...]).astype(BF16)

    for c in range(0, d_model, n):
        y_a = jnp.dot(act_scr[...], wcp_ref[:, c:c + n], preferred_element_type=F32) + cpb_ref[:, c:c + n]
        g_a = _sigmoid(mm(off_gt + c, n) + gb_ref[:, c:c + n])
        gya_ref[:, c:c + n] = (g_a * y_a).astype(BF16)


def _mixer_in(x2, norm_g, w_in, gate_b, cos, sin, dw_w, dw_b, ln_g, ln_b, w_cp, cp_b,
              *, batch, seq, heads, tm=512):
    tokens, d_model = x2.shape
    half = RET_DK // 2
    s_tiles = seq // tm
    assert dw_w.shape[0] - 1 <= CONV_HALO <= tm
    tok = lambda b, s: (b * s_tiles + s, 0)
    pos = lambda b, s: (s, 0)
    out_width = _mixer_sections(d_model, heads)[-1][1]
    return pl.pallas_call(
        functools.partial(_mixer_in_body, tm=tm, d_model=d_model, heads=heads),
        grid=(batch, s_tiles),
        in_specs=[
            pl.BlockSpec((tm, d_model), tok),
            _resident(norm_g.shape), _resident(w_in.shape), _resident(gate_b.shape),
            pl.BlockSpec((tm, half), pos), pl.BlockSpec((tm, half), pos),
            _resident(dw_w.shape), _resident(dw_b.shape), _resident(ln_g.shape), _resident(ln_b.shape),
            _resident(w_cp.shape), _resident(cp_b.shape),
        ],
        out_specs=pl.BlockSpec((tm, out_width), tok),
        out_shape=jax.ShapeDtypeStruct((tokens, out_width), BF16),
        scratch_shapes=[
            pltpu.VMEM((tm, d_model), BF16),
            pltpu.VMEM(_conv_buf_shape(d_model, CONV_HALO, tm), F32),
            pltpu.VMEM((tm, d_model), BF16),
        ],
        compiler_params=pltpu.CompilerParams(
            dimension_semantics=("arbitrary", "arbitrary"), vmem_limit_bytes=VMEM_LIMIT),
        name="mixer_in",
    )(x2, norm_g, w_in, gate_b, cos, sin, dw_w, dw_b, ln_g, ln_b, w_cp, cp_b)


def _retention_body(cdec_ref, q_ref, k_ref, v_ref, sg_ref, inner_ref, cross_ref, state_ref, rg_ref,
                    o_ref, r_scr, *, chunks, unroll):
    r_scr[...] = jnp.zeros_like(r_scr)
    chunk_decay = cdec_ref[pl.program_id(1)]
    c_len = RET_CHUNK

    def chunk_group(i, carry):
        base = pl.multiple_of(i * (unroll * c_len), unroll * c_len)
        for c in range(unroll):
            rows = pl.ds(base + c * c_len, c_len)
            qc = q_ref[rows, :]
            kc = k_ref[rows, :]
            vc = v_ref[rows, :]
            state = r_scr[...]
            s = lax.dot_general(qc, kc, (((1,), (1,)), ((), ())), preferred_element_type=F32) * inner_ref[0]
            inner = jnp.dot(s.astype(BF16), vc, preferred_element_type=F32)
            cross = jnp.dot(qc, state.astype(BF16), preferred_element_type=F32) * cross_ref[0]
            kw = (kc.astype(F32) * state_ref[0]).astype(BF16)
            r_scr[...] = state * chunk_decay + lax.dot_general(
                kw, vc, (((0,), (0,)), ((), ())), preferred_element_type=F32)
            r = inner + cross
            mu = jnp.mean(r, axis=-1, keepdims=True)
            d = r - mu
            var = jnp.mean(d * d, axis=-1, keepdims=True)
            rn = d * lax.rsqrt(var + EPS) * rg_ref[...]
            o_ref[rows, :] = (sg_ref[rows, :].astype(F32) * rn).astype(BF16)
        return carry

    lax.fori_loop(0, chunks // unroll, chunk_group, 0)


def _retention(mixed, ret_norm_g, *, batch, seq, heads, d_model, unroll=4):
    tokens = mixed.shape[0]
    c_len = RET_CHUNK
    assert seq % (unroll * c_len) == 0
    (q0, _), (k0, _), (v0, _), (g0, _) = _mixer_sections(d_model, heads)[:4]
    assert q0 % RET_DK == k0 % RET_DK == v0 % RET_DV == g0 % RET_DV == 0
    head_block = lambda col0, width: (lambda b, h: (b, col0 // width + h))
    log_gamma = jnp.log1p(-jnp.power(2.0, -5.0 - jnp.arange(heads, dtype=F32)))
    idx = jnp.arange(c_len, dtype=F32)
    diff = idx[:, None] - idx[None, :]
    causal = diff >= 0
    inner_decay = jnp.where(
        causal[None], jnp.exp(jnp.where(causal, diff, 0.0)[None] * log_gamma[:, None, None]), 0.0)
    cross_decay = jnp.exp((idx + 1.0)[None, :] * log_gamma[:, None])[:, :, None]
    state_decay = jnp.exp((c_len - 1.0 - idx)[None, :] * log_gamma[:, None])[:, :, None]
    chunk_decay = jnp.exp(c_len * log_gamma)

    seq_head = lambda b, h: (b, h)
    per_head = lambda b, h: (h, 0, 0)
    return pl.pallas_call(
        functools.partial(_retention_body, chunks=seq // c_len, unroll=unroll),
        grid=(batch, heads),
        in_specs=[
            pl.BlockSpec(memory_space=pltpu.SMEM),
            pl.BlockSpec((seq, RET_DK), head_block(q0, RET_DK)),
            pl.BlockSpec((seq, RET_DK), head_block(k0, RET_DK)),
            pl.BlockSpec((seq, RET_DV), head_block(v0, RET_DV)),
            pl.BlockSpec((seq, RET_DV), head_block(g0, RET_DV)),
            pl.BlockSpec((1, c_len, c_len), per_head),
            pl.BlockSpec((1, c_len, 1), per_head),
            pl.BlockSpec((1, c_len, 1), per_head),
            pl.BlockSpec((1, RET_DV), lambda b, h: (0, h)),
        ],
        out_specs=pl.BlockSpec((seq, RET_DV), seq_head),
        out_shape=jax.ShapeDtypeStruct((tokens, heads * RET_DV), BF16),
        scratch_shapes=[pltpu.VMEM((RET_DK, RET_DV), F32)],
        compiler_params=pltpu.CompilerParams(
            dimension_semantics=("arbitrary", "arbitrary"), vmem_limit_bytes=VMEM_LIMIT),
        name="retention",
    )(chunk_decay, mixed, mixed, mixed, mixed, inner_decay, cross_decay, state_decay, ret_norm_g)


FFN_HALO = SUBLANES


def _post_body(x_ref, r_ref, gya_ref, gtb_ref, wrp_ref, wout_ref, ng_ref, wup_ref, dww_ref, dwb_ref,
               wdn_ref, fg_ref, o_ref, x1_scr, h_scr, carry, ubuf, ff_scr,
               *, tm, ffn_dim, fc, row_block, final_norm):
    @pl.when(pl.program_id(1) == 0)
    def _():
        carry[...] = jnp.zeros_like(carry)

    y_b = jnp.dot(r_ref[...], wrp_ref[...], preferred_element_type=F32)
    mix = (gya_ref[...].astype(F32) + gtb_ref[...].astype(F32) * y_b).astype(BF16)
    x1 = x_ref[...] + jnp.dot(mix, wout_ref[...], preferred_element_type=F32)
    x1_scr[...] = x1
    h_scr[...] = _rms_norm(x1, ng_ref[...]).astype(BF16)

    for j, c in enumerate(range(0, ffn_dim, fc)):
        for half_idx, off in enumerate((c, ffn_dim + c)):
            ub = ubuf.at[j % 2, half_idx]
            ci = 2 * j + half_idx
            ub[:, 0:2 * FFN_HALO, :] = carry[ci]
            _conv_buf_store(ub, FFN_HALO,
                            jnp.dot(h_scr[...], wup_ref[:, off:off + fc], preferred_element_type=F32))
            carry[ci] = ub[:, 2 * tm:2 * (tm + FFN_HALO), :]
        for r0 in range(0, tm, row_block):
            for s in range(fc // LANES):
                u_act, u_lin = (
                    _conv_block(ubuf.at[j % 2, half_idx], FFN_HALO, dww_ref, dwb_ref, r0, row_block, s,
                                off + s * LANES)
                    for half_idx, off in enumerate((c, ffn_dim + c)))
                ff_scr[r0:r0 + row_block, c + s * LANES:c + (s + 1) * LANES] = (
                    _silu(u_act) * u_lin).astype(BF16)

    y = x1_scr[...] + jnp.dot(ff_scr[...], wdn_ref[...], preferred_element_type=F32)
    if final_norm:
        y = _rms_norm(y, fg_ref[...])
    o_ref[...] = y


def _post(x2, r, mixed, w_rp, w_out, norm_g, w_up, dw_w, dw_b, w_down, final_g,
          *, batch, seq, heads, final_norm, tm=512, fc=512):
    tokens, d_model = x2.shape
    ffn_dim = w_down.shape[0]
    s_tiles = seq // tm
    assert dw_w.shape[0] - 1 <= FFN_HALO
    tok = lambda b, s: (b * s_tiles + s, 0)
    (gya0, _), (gtb0, _) = _mixer_sections(d_model, heads)[4:]
    assert gya0 % d_model == gtb0 % d_model == 0
    tok_section = lambda col0: (lambda b, s: (b * s_tiles + s, col0 // d_model))
    return pl.pallas_call(
        functools.partial(_post_body, tm=tm, ffn_dim=ffn_dim, fc=fc, row_block=32, final_norm=final_norm),
        grid=(batch, s_tiles),
        in_specs=[
            pl.BlockSpec((tm, d_model), tok),
            pl.BlockSpec((tm, r.shape[1]), tok),
            pl.BlockSpec((tm, d_model), tok_section(gya0)),
            pl.BlockSpec((tm, d_model), tok_section(gtb0)),
            _resident(w_rp.shape), _resident(w_out.shape),
            _resident(norm_g.shape), _resident(w_up.shape), _resident(dw_w.shape), _resident(dw_b.shape),
            _resident(w_down.shape), _resident(final_g.shape),
        ],
        out_specs=pl.BlockSpec((tm, d_model), tok),
        out_shape=jax.ShapeDtypeStruct((tokens, d_model), F32),
        scratch_shapes=[
            pltpu.VMEM((tm, d_model), F32),
            pltpu.VMEM((tm, d_model), BF16),
            pltpu.VMEM((2 * ffn_dim // fc,) + _conv_buf_shape(fc, FFN_HALO, 0), F32),
            pltpu.VMEM((2, 2) + _conv_buf_shape(fc, FFN_HALO, tm), F32),
            pltpu.VMEM((tm, ffn_dim), BF16),
        ],
        compiler_params=pltpu.CompilerParams(
            dimension_semantics=("arbitrary", "arbitrary"), vmem_limit_bytes=VMEM_LIMIT),
        name="post",
    )(x2, r, mixed, mixed, w_rp, w_out, norm_g, w_up, dw_w, dw_b, w_down, final_g)


def kernel(x, norm_mix_g, w_in, gate_b, conv_dw_w, conv_dw_b, conv_ln_g, conv_ln_b, w_conv_proj,
           conv_proj_b, ret_norm_g, w_ret_proj, w_out, norm_ffn_g, w_up, ffn_dw_w, ffn_dw_b, w_down,
           norm_final_g):
    batch, seq, d_model = x.shape
    depth = w_in.shape[0]
    heads = d_model // RET_DK
    half = RET_DK // 2

    pos = jnp.arange(seq, dtype=F32)
    inv_freq = ROPE_BASE ** (-jnp.arange(half, dtype=F32) / half)
    ang = pos[:, None] * inv_freq[None, :]
    cos, sin = jnp.cos(ang), jnp.sin(ang)

    row = lambda v: v.reshape(1, -1)
    x2 = x.reshape(batch * seq, d_model)
    for l in range(depth):
        mixed = _mixer_in(
            x2, row(norm_mix_g[l]), w_in[l].astype(BF16), row(gate_b[l]), cos, sin,
            conv_dw_w[l], row(conv_dw_b[l]), row(conv_ln_g[l]), row(conv_ln_b[l]),
            w_conv_proj[l].astype(BF16), row(conv_proj_b[l]), batch=batch, seq=seq, heads=heads)
        r = _retention(mixed, row(ret_norm_g[l]), batch=batch, seq=seq, heads=heads, d_model=d_model)
        x2 = _post(
            x2, r, mixed, w_ret_proj[l].astype(BF16), w_out[l].astype(BF16),
            row(norm_ffn_g[l]), w_up[l].astype(BF16), ffn_dw_w[l], row(ffn_dw_b[l]),
            w_down[l].astype(BF16), row(norm_final_g), batch=batch, seq=seq, heads=heads,
            final_norm=(l == depth - 1))
    return x2.reshape(batch, seq, d_model)
```

```python
import functools

import jax
import jax.numpy as jnp
from jax import lax
from jax.experimental import pallas as pl
from jax.experimental.pallas import tpu as pltpu

EPS = 1e-6
ROPE_BASE = 10000.0
RET_DK = 256
RET_DV = 2 * RET_DK
RET_CHUNK = 256
LANES = 128
MXU_COLS = 256
PROJ_COLS = 2 * MXU_COLS
SUBLANES = 8
BF16_ROWS = 2 * SUBLANES
MIB = 1024 * 1024
VMEM_LIMIT = 56 * MIB

BF16 = jnp.bfloat16
F32 = jnp.float32


def _sigmoid(x):
    return 1.0 / (1.0 + jnp.exp(-x))


def _silu(x):
    return x * _sigmoid(x)


def _rms_norm(x, g):
    ms = jnp.mean(x * x, axis=-1, keepdims=True)
    return x * lax.rsqrt(ms + EPS) * g


def _resident(shape):
    nd = len(shape)
    return pl.BlockSpec(shape, lambda *_: (0,) * nd, pipeline_mode=pl.Buffered(1))


def _conv_buf_shape(width, halo, rows):
    return (width // (2 * LANES), 2 * (halo + rows), LANES)


def _conv_buf_store(buf, halo, val, slab0=0):
    rows, width = val.shape
    for i in range(width // LANES):
        s = slab0 + i
        buf[s // 2, pl.ds(2 * halo + s % 2, rows, stride=2), :] = val[:, i * LANES:(i + 1) * LANES]


def _conv_block(buf, halo, w_ref, b_ref, r0, rows, slab, col0):
    taps = w_ref.shape[0]
    base = halo - (taps - 1)
    lanes = slice(col0, col0 + LANES)
    acc = jnp.broadcast_to(b_ref[:, lanes], (rows, LANES))
    for k in range(taps):
        src = buf[slab // 2, pl.ds(2 * (r0 + base + k) + slab % 2, rows, stride=2), :]
        acc = acc + src * w_ref[k:k + 1, lanes]
    return acc


def _conv_halo_shift(buf, halo, rows, first):
    @pl.when(first)
    def _():
        buf[:, 0:2 * halo, :] = jnp.zeros((buf.shape[0], 2 * halo, LANES), F32)

    @pl.when(jnp.logical_not(first))
    def _():
        buf[:, 0:2 * halo, :] = buf[:, 2 * rows:2 * (rows + halo), :]


CONV_HALO = 32


def _mixer_sections(d_model, heads):
    widths = (heads * RET_DK, heads * RET_DK, heads * RET_DV, heads * RET_DV, d_model, d_model)
    edges = [0]
    for w in widths:
        edges.append(edges[-1] + w)
    return tuple(zip(edges[:-1], edges[1:]))


def _mixer_in_body(x_ref, ng_ref, w_ref, gb_ref, cos_ref, sin_ref, dww_ref, dwb_ref, lng_ref, lnb_ref,
                   wcp_ref, cpb_ref, rg_ref, out_ref, h_scr, abuf, act_scr, *, tm, d_model, heads):
    q_ref, k_ref, v_ref, sg_ref, gya_ref, gtb_ref = (
        out_ref.at[:, lo:hi] for lo, hi in _mixer_sections(d_model, heads))
    conv_ch = d_model
    off_q = 2 * conv_ch
    off_k = off_q + heads * RET_DK
    off_v = off_k + heads * RET_DK
    off_g = off_v + heads * RET_DV
    off_gt = off_g + heads * RET_DV

    _conv_halo_shift(abuf, CONV_HALO, tm, pl.program_id(1) == 0)
    h_scr[...] = _rms_norm(x_ref[...], ng_ref[...]).astype(BF16)

    def mm(c0, n):
        return jnp.dot(h_scr[...], w_ref[:, c0:c0 + n], preferred_element_type=F32)

    for c in range(0, conv_ch, MXU_COLS):
        _conv_buf_store(abuf, CONV_HALO, mm(c, MXU_COLS) * _sigmoid(mm(conv_ch + c, MXU_COLS)),
                        slab0=c // LANES)

    cos = cos_ref[...]
    sin = sin_ref[...]
    half = RET_DK // 2

    for dst_ref, off, scale in ((q_ref, off_q, RET_DK ** -0.5), (k_ref, off_k, None)):
        for hd in range(heads):
            y = mm(off + hd * RET_DK, RET_DK)
            y1, y2 = y[:, :half], y[:, half:]
            r1 = y1 * cos - y2 * sin
            r2 = y2 * cos + y1 * sin
            if scale is not None:
                r1, r2 = r1 * scale, r2 * scale
            dst_ref[:, hd * RET_DK:hd * RET_DK + half] = r1.astype(BF16)
            dst_ref[:, hd * RET_DK + half:(hd + 1) * RET_DK] = r2.astype(BF16)

    n = PROJ_COLS
    for c in range(0, heads * RET_DV, n):
        v_ref[:, c:c + n] = mm(off_v + c, n).astype(BF16)
    for c in range(0, heads * RET_DV, n):
        sg_ref[:, c:c + n] = (_silu(mm(off_g + c, n)) * rg_ref[:, c:c + n]).astype(BF16)
    for c in range(0, d_model, n):
        gtb_ref[:, c:c + n] = _sigmoid(
            mm(off_gt + d_model + c, n) + gb_ref[:, d_model + c:d_model + c + n]).astype(BF16)

    for r0 in range(0, tm, BF16_ROWS):
        c = jnp.concatenate(
            [_conv_block(abuf, CONV_HALO, dww_ref, dwb_ref, r0, BF16_ROWS, s, s * LANES)
             for s in range(conv_ch // LANES)], axis=-1)
        mu = jnp.mean(c, axis=-1, keepdims=True)
        d = c - mu
        var = jnp.mean(d * d, axis=-1, keepdims=True)
        act_scr[r0:r0 + BF16_ROWS, :] = _silu(
            d * lax.rsqrt(var + EPS) * lng_ref[...] + lnb_ref[...]).astype(BF16)

    for c in range(0, d_model, n):
        y_a = jnp.dot(act_scr[...], wcp_ref[:, c:c + n], preferred_element_type=F32) + cpb_ref[:, c:c + n]
        g_a = _sigmoid(mm(off_gt + c, n) + gb_ref[:, c:c + n])
        gya_ref[:, c:c + n] = (g_a * y_a).astype(BF16)


def _mixer_in(x2, norm_g, w_in, gate_b, cos, sin, dw_w, dw_b, ln_g, ln_b, w_cp, cp_b, ret_g,
              *, batch, seq, heads, tm=512):
    tokens, d_model = x2.shape
    half = RET_DK // 2
    s_tiles = seq // tm
    assert dw_w.shape[0] - 1 <= CONV_HALO <= tm
    tok = lambda b, s: (b * s_tiles + s, 0)
    pos = lambda b, s: (s, 0)
    out_width = _mixer_sections(d_model, heads)[-1][1]
    return pl.pallas_call(
        functools.partial(_mixer_in_body, tm=tm, d_model=d_model, heads=heads),
        grid=(batch, s_tiles),
        in_specs=[
            pl.BlockSpec((tm, d_model), tok),
            _resident(norm_g.shape), _resident(w_in.shape), _resident(gate_b.shape),
            pl.BlockSpec((tm, half), pos), pl.BlockSpec((tm, half), pos),
            _resident(dw_w.shape), _resident(dw_b.shape), _resident(ln_g.shape), _resident(ln_b.shape),
            _resident(w_cp.shape), _resident(cp_b.shape), _resident(ret_g.shape),
        ],
        out_specs=pl.BlockSpec((tm, out_width), tok),
        out_shape=jax.ShapeDtypeStruct((tokens, out_width), BF16),
        scratch_shapes=[
            pltpu.VMEM((tm, d_model), BF16),
            pltpu.VMEM(_conv_buf_shape(d_model, CONV_HALO, tm), F32),
            pltpu.VMEM((tm, d_model), BF16),
        ],
        compiler_params=pltpu.CompilerParams(
            dimension_semantics=("arbitrary", "arbitrary"), vmem_limit_bytes=VMEM_LIMIT),
        name="mixer_in",
    )(x2, norm_g, w_in, gate_b, cos, sin, dw_w, dw_b, ln_g, ln_b, w_cp, cp_b, ret_g)


def _retention_body(cdec_ref, q_ref, k_ref, v_ref, sg_ref, inner_ref, cross_ref, state_ref,
                    o_ref, r_scr, *, chunks, unroll):
    r_scr[...] = jnp.zeros_like(r_scr)
    chunk_decay = cdec_ref[pl.program_id(1)]
    c_len = RET_CHUNK

    def chunk_group(i, carry):
        base = pl.multiple_of(i * (unroll * c_len), unroll * c_len)
        for c in range(unroll):
            rows = pl.ds(base + c * c_len, c_len)
            qc = q_ref[rows, :]
            kc = k_ref[rows, :]
            vc = v_ref[rows, :]
            state = r_scr[...]
            s = lax.dot_general(qc, kc, (((1,), (1,)), ((), ())), preferred_element_type=F32) * inner_ref[0]
            inner = jnp.dot(s.astype(BF16), vc, preferred_element_type=F32)
            cross = jnp.dot(qc, state.astype(BF16), preferred_element_type=F32) * cross_ref[0]
            kw = (kc.astype(F32) * state_ref[0]).astype(BF16)
            r_scr[...] = state * chunk_decay + lax.dot_general(
                kw, vc, (((0,), (0,)), ((), ())), preferred_element_type=F32)
            r = inner + cross
            mu = jnp.mean(r, axis=-1, keepdims=True)
            d = r - mu
            var = jnp.mean(d * d, axis=-1, keepdims=True)
            rn = d * lax.rsqrt(var + EPS)
            o_ref[rows, :] = (sg_ref[rows, :].astype(F32) * rn).astype(BF16)
        return carry

    lax.fori_loop(0, chunks // unroll, chunk_group, 0)


def _retention(mixed, *, batch, seq, heads, d_model, unroll=4):
    tokens = mixed.shape[0]
    c_len = RET_CHUNK
    assert seq % (unroll * c_len) == 0
    (q0, _), (k0, _), (v0, _), (g0, _) = _mixer_sections(d_model, heads)[:4]
    assert q0 % RET_DK == k0 % RET_DK == v0 % RET_DV == g0 % RET_DV == 0
    head_block = lambda col0, width: (lambda b, h: (b, col0 // width + h))
    log_gamma = jnp.log1p(-jnp.power(2.0, -5.0 - jnp.arange(heads, dtype=F32)))
    idx = jnp.arange(c_len, dtype=F32)
    diff = idx[:, None] - idx[None, :]
    causal = diff >= 0
    inner_decay = jnp.where(
        causal[None], jnp.exp(jnp.where(causal, diff, 0.0)[None] * log_gamma[:, None, None]), 0.0)
    cross_decay = jnp.exp((idx + 1.0)[None, :] * log_gamma[:, None])[:, :, None]
    state_decay = jnp.exp((c_len - 1.0 - idx)[None, :] * log_gamma[:, None])[:, :, None]
    chunk_decay = jnp.exp(c_len * log_gamma)

    seq_head = lambda b, h: (b, h)
    per_head = lambda b, h: (h, 0, 0)
    return pl.pallas_call(
        functools.partial(_retention_body, chunks=seq // c_len, unroll=unroll),
        grid=(batch, heads),
        in_specs=[
            pl.BlockSpec(memory_space=pltpu.SMEM),
            pl.BlockSpec((seq, RET_DK), head_block(q0, RET_DK)),
            pl.BlockSpec((seq, RET_DK), head_block(k0, RET_DK)),
            pl.BlockSpec((seq, RET_DV), head_block(v0, RET_DV)),
            pl.BlockSpec((seq, RET_DV), head_block(g0, RET_DV)),
            pl.BlockSpec((1, c_len, c_len), per_head),
            pl.BlockSpec((1, c_len, 1), per_head),
            pl.BlockSpec((1, c_len, 1), per_head),
        ],
        out_specs=pl.BlockSpec((seq, RET_DV), seq_head),
        out_shape=jax.ShapeDtypeStruct((tokens, heads * RET_DV), BF16),
        scratch_shapes=[pltpu.VMEM((RET_DK, RET_DV), F32)],
        compiler_params=pltpu.CompilerParams(
            dimension_semantics=("arbitrary", "arbitrary"), vmem_limit_bytes=VMEM_LIMIT),
        name="retention",
    )(chunk_decay, mixed, mixed, mixed, mixed, inner_decay, cross_decay, state_decay)


FFN_HALO = SUBLANES


def _post_body(x_ref, r_ref, gya_ref, gtb_ref, wrp_ref, wout_ref, ng_ref, wup_ref, dww_ref, dwb_ref,
               wdn_ref, fg_ref, o_ref, x1_scr, h_scr, carry, ubuf, ff_scr,
               *, tm, ffn_dim, fc, row_block, final_norm):
    @pl.when(pl.program_id(1) == 0)
    def _():
        carry[...] = jnp.zeros_like(carry)

    y_b = jnp.dot(r_ref[...], wrp_ref[...], preferred_element_type=F32)
    mix = (gya_ref[...].astype(F32) + gtb_ref[...].astype(F32) * y_b).astype(BF16)
    x1 = x_ref[...] + jnp.dot(mix, wout_ref[...], preferred_element_type=F32)
    x1_scr[...] = x1
    h_scr[...] = _rms_norm(x1, ng_ref[...]).astype(BF16)

    for j, c in enumerate(range(0, ffn_dim, fc)):
        for half_idx, off in enumerate((c, ffn_dim + c)):
            ub = ubuf.at[j % 2, half_idx]
            ci = 2 * j + half_idx
            ub[:, 0:2 * FFN_HALO, :] = carry[ci]
            _conv_buf_store(ub, FFN_HALO,
                            jnp.dot(h_scr[...], wup_ref[:, off:off + fc], preferred_element_type=F32))
            carry[ci] = ub[:, 2 * tm:2 * (tm + FFN_HALO), :]
        for r0 in range(0, tm, row_block):
            for s in range(fc // LANES):
                u_act, u_lin = (
                    _conv_block(ubuf.at[j % 2, half_idx], FFN_HALO, dww_ref, dwb_ref, r0, row_block, s,
                                off + s * LANES)
                    for half_idx, off in enumerate((c, ffn_dim + c)))
                ff_scr[r0:r0 + row_block, c + s * LANES:c + (s + 1) * LANES] = (
                    _silu(u_act) * u_lin).astype(BF16)

    y = x1_scr[...] + jnp.dot(ff_scr[...], wdn_ref[...], preferred_element_type=F32)
    if final_norm:
        y = _rms_norm(y, fg_ref[...])
    o_ref[...] = y


def _post(x2, r, mixed, w_rp, w_out, norm_g, w_up, dw_w, dw_b, w_down, final_g,
          *, batch, seq, heads, final_norm, tm=512, fc=512):
    tokens, d_model = x2.shape
    ffn_dim = w_down.shape[0]
    s_tiles = seq // tm
    assert dw_w.shape[0] - 1 <= FFN_HALO
    tok = lambda b, s: (b * s_tiles + s, 0)
    (gya0, _), (gtb0, _) = _mixer_sections(d_model, heads)[4:]
    assert gya0 % d_model == gtb0 % d_model == 0
    tok_section = lambda col0: (lambda b, s: (b * s_tiles + s, col0 // d_model))
    return pl.pallas_call(
        functools.partial(_post_body, tm=tm, ffn_dim=ffn_dim, fc=fc, row_block=32, final_norm=final_norm),
        grid=(batch, s_tiles),
        in_specs=[
            pl.BlockSpec((tm, d_model), tok),
            pl.BlockSpec((tm, r.shape[1]), tok),
            pl.BlockSpec((tm, d_model), tok_section(gya0)),
            pl.BlockSpec((tm, d_model), tok_section(gtb0)),
            _resident(w_rp.shape), _resident(w_out.shape),
            _resident(norm_g.shape), _resident(w_up.shape), _resident(dw_w.shape), _resident(dw_b.shape),
            _resident(w_down.shape), _resident(final_g.shape),
        ],
        out_specs=pl.BlockSpec((tm, d_model), tok),
        out_shape=jax.ShapeDtypeStruct((tokens, d_model), F32),
        scratch_shapes=[
            pltpu.VMEM((tm, d_model), F32),
            pltpu.VMEM((tm, d_model), BF16),
            pltpu.VMEM((2 * ffn_dim // fc,) + _conv_buf_shape(fc, FFN_HALO, 0), F32),
            pltpu.VMEM((2, 2) + _conv_buf_shape(fc, FFN_HALO, tm), F32),
            pltpu.VMEM((tm, ffn_dim), BF16),
        ],
        compiler_params=pltpu.CompilerParams(
            dimension_semantics=("arbitrary", "arbitrary"), vmem_limit_bytes=VMEM_LIMIT),
        name="post",
    )(x2, r, mixed, mixed, w_rp, w_out, norm_g, w_up, dw_w, dw_b, w_down, final_g)


def kernel(x, norm_mix_g, w_in, gate_b, conv_dw_w, conv_dw_b, conv_ln_g, conv_ln_b, w_conv_proj,
           conv_proj_b, ret_norm_g, w_ret_proj, w_out, norm_ffn_g, w_up, ffn_dw_w, ffn_dw_b, w_down,
           norm_final_g):
    batch, seq, d_model = x.shape
    depth = w_in.shape[0]
    heads = d_model // RET_DK
    half = RET_DK // 2

    pos = jnp.arange(seq, dtype=F32)
    inv_freq = ROPE_BASE ** (-jnp.arange(half, dtype=F32) / half)
    ang = pos[:, None] * inv_freq[None, :]
    cos, sin = jnp.cos(ang), jnp.sin(ang)

    row = lambda v: v.reshape(1, -1)
    x2 = x.reshape(batch * seq, d_model)
    for l in range(depth):
        mixed = _mixer_in(
            x2, row(norm_mix_g[l]), w_in[l].astype(BF16), row(gate_b[l]), cos, sin,
            conv_dw_w[l], row(conv_dw_b[l]), row(conv_ln_g[l]), row(conv_ln_b[l]),
            w_conv_proj[l].astype(BF16), row(conv_proj_b[l]), row(ret_norm_g[l]),
            batch=batch, seq=seq, heads=heads)
        r = _retention(mixed, batch=batch, seq=seq, heads=heads, d_model=d_model)
        x2 = _post(
            x2, r, mixed, w_ret_proj[l].astype(BF16), w_out[l].astype(BF16),
            row(norm_ffn_g[l]), w_up[l].astype(BF16), ffn_dw_w[l], row(ffn_dw_b[l]),
            w_down[l].astype(BF16), row(norm_final_g), batch=batch, seq=seq, heads=heads,
            final_norm=(l == depth - 1))
    return x2.reshape(batch, seq, d_model)
```

```python
import functools

import jax
import jax.numpy as jnp
from jax import lax
from jax.experimental import pallas as pl
from jax.experimental.pallas import tpu as pltpu

EPS = 1e-6
ROPE_BASE = 10000.0
RET_DK = 256
RET_DV = 2 * RET_DK
RET_CHUNK = 256
LANES = 128
MXU_COLS = 256
PROJ_COLS = 2 * MXU_COLS
SUBLANES = 8
BF16_ROWS = 2 * SUBLANES
MIB = 1024 * 1024
VMEM_LIMIT = 56 * MIB

BF16 = jnp.bfloat16
F32 = jnp.float32


def _sigmoid(x):
    return 1.0 / (1.0 + jnp.exp(-x))


def _silu(x):
    return x * _sigmoid(x)


def _rms_norm(x, g):
    ms = jnp.mean(x * x, axis=-1, keepdims=True)
    return x * lax.rsqrt(ms + EPS) * g


def _resident(shape):
    nd = len(shape)
    return pl.BlockSpec(shape, lambda *_: (0,) * nd, pipeline_mode=pl.Buffered(1))


def _conv_buf_shape(width, halo, rows):
    return (width // (2 * LANES), 2 * (halo + rows), LANES)


def _conv_buf_store(buf, halo, val, slab0=0):
    rows, width = val.shape
    for i in range(width // LANES):
        s = slab0 + i
        buf[s // 2, pl.ds(2 * halo + s % 2, rows, stride=2), :] = val[:, i * LANES:(i + 1) * LANES]


def _conv_block(buf, halo, w_ref, b_ref, r0, rows, slab, col0):
    taps = w_ref.shape[0]
    base = halo - (taps - 1)
    lanes = slice(col0, col0 + LANES)
    acc = jnp.broadcast_to(b_ref[:, lanes], (rows, LANES))
    for k in range(taps):
        src = buf[slab // 2, pl.ds(2 * (r0 + base + k) + slab % 2, rows, stride=2), :]
        acc = acc + src * w_ref[k:k + 1, lanes]
    return acc


def _conv_halo_shift(buf, halo, rows, first):
    @pl.when(first)
    def _():
        buf[:, 0:2 * halo, :] = jnp.zeros((buf.shape[0], 2 * halo, LANES), F32)

    @pl.when(jnp.logical_not(first))
    def _():
        buf[:, 0:2 * halo, :] = buf[:, 2 * rows:2 * (rows + halo), :]


CONV_HALO = 32


def _mixer_sections(d_model, heads):
    widths = (heads * RET_DK, heads * RET_DK, heads * RET_DV, heads * RET_DV, d_model, d_model)
    edges = [0]
    for w in widths:
        edges.append(edges[-1] + w)
    return tuple(zip(edges[:-1], edges[1:]))


def _mixer_in_body(x_ref, ng_ref, w_ref, gb_ref, cos_ref, sin_ref, dww_ref, dwb_ref, lng_ref, lnb_ref,
                   wcp_ref, cpb_ref, rg_ref, out_ref, h_scr, abuf, act_scr, *, tm, d_model, heads):
    q_ref, k_ref, v_ref, sg_ref, gya_ref, gtb_ref = (
        out_ref.at[:, lo:hi] for lo, hi in _mixer_sections(d_model, heads))
    conv_ch = d_model
    off_q = 2 * conv_ch
    off_k = off_q + heads * RET_DK
    off_v = off_k + heads * RET_DK
    off_g = off_v + heads * RET_DV
    off_gt = off_g + heads * RET_DV

    _conv_halo_shift(abuf, CONV_HALO, tm, pl.program_id(1) == 0)
    h_scr[...] = _rms_norm(x_ref[...], ng_ref[...]).astype(BF16)

    def mm(c0, n):
        return jnp.dot(h_scr[...], w_ref[:, c0:c0 + n], preferred_element_type=F32)

    for c in range(0, conv_ch, MXU_COLS):
        _conv_buf_store(abuf, CONV_HALO, mm(c, MXU_COLS) * _sigmoid(mm(conv_ch + c, MXU_COLS)),
                        slab0=c // LANES)

    cos = cos_ref[...]
    sin = sin_ref[...]
    half = RET_DK // 2

    for dst_ref, off, scale in ((q_ref, off_q, RET_DK ** -0.5), (k_ref, off_k, None)):
        for hd in range(heads):
            y = mm(off + hd * RET_DK, RET_DK)
            y1, y2 = y[:, :half], y[:, half:]
            r1 = y1 * cos - y2 * sin
            r2 = y2 * cos + y1 * sin
            if scale is not None:
                r1, r2 = r1 * scale, r2 * scale
            dst_ref[:, hd * RET_DK:hd * RET_DK + half] = r1.astype(BF16)
            dst_ref[:, hd * RET_DK + half:(hd + 1) * RET_DK] = r2.astype(BF16)

    n = PROJ_COLS
    for c in range(0, heads * RET_DV, n):
        v_ref[:, c:c + n] = mm(off_v + c, n).astype(BF16)
    for c in range(0, heads * RET_DV, n):
        sg_ref[:, c:c + n] = (_silu(mm(off_g + c, n)) * rg_ref[:, c:c + n]).astype(BF16)
    for c in range(0, d_model, n):
        gtb_ref[:, c:c + n] = _sigmoid(
            mm(off_gt + d_model + c, n) + gb_ref[:, d_model + c:d_model + c + n]).astype(BF16)

    for r0 in range(0, tm, BF16_ROWS):
        c = jnp.concatenate(
            [_conv_block(abuf, CONV_HALO, dww_ref, dwb_ref, r0, BF16_ROWS, s, s * LANES)
             for s in range(conv_ch // LANES)], axis=-1)
        mu = jnp.mean(c, axis=-1, keepdims=True)
        d = c - mu
        var = jnp.mean(d * d, axis=-1, keepdims=True)
        act_scr[r0:r0 + BF16_ROWS, :] = _silu(
            d * lax.rsqrt(var + EPS) * lng_ref[...] + lnb_ref[...]).astype(BF16)

    for c in range(0, d_model, n):
        y_a = jnp.dot(act_scr[...], wcp_ref[:, c:c + n], preferred_element_type=F32) + cpb_ref[:, c:c + n]
        g_a = _sigmoid(mm(off_gt + c, n) + gb_ref[:, c:c + n])
        gya_ref[:, c:c + n] = (g_a * y_a).astype(BF16)


def _mixer_in(x2, norm_g, w_in, gate_b, cos, sin, dw_w, dw_b, ln_g, ln_b, w_cp, cp_b, ret_g,
              *, batch, seq, heads, tm=512):
    tokens, d_model = x2.shape
    half = RET_DK // 2
    s_tiles = seq // tm
    assert dw_w.shape[0] - 1 <= CONV_HALO <= tm
    tok = lambda b, s: (b * s_tiles + s, 0)
    pos = lambda b, s: (s, 0)
    out_width = _mixer_sections(d_model, heads)[-1][1]
    return pl.pallas_call(
        functools.partial(_mixer_in_body, tm=tm, d_model=d_model, heads=heads),
        grid=(batch, s_tiles),
        in_specs=[
            pl.BlockSpec((tm, d_model), tok),
            _resident(norm_g.shape), _resident(w_in.shape), _resident(gate_b.shape),
            pl.BlockSpec((tm, half), pos), pl.BlockSpec((tm, half), pos),
            _resident(dw_w.shape), _resident(dw_b.shape), _resident(ln_g.shape), _resident(ln_b.shape),
            _resident(w_cp.shape), _resident(cp_b.shape), _resident(ret_g.shape),
        ],
        out_specs=pl.BlockSpec((tm, out_width), tok),
        out_shape=jax.ShapeDtypeStruct((tokens, out_width), BF16),
        scratch_shapes=[
            pltpu.VMEM((tm, d_model), BF16),
            pltpu.VMEM(_conv_buf_shape(d_model, CONV_HALO, tm), F32),
            pltpu.VMEM((tm, d_model), BF16),
        ],
        compiler_params=pltpu.CompilerParams(
            dimension_semantics=("arbitrary", "arbitrary"), vmem_limit_bytes=VMEM_LIMIT),
        name="mixer_in",
    )(x2, norm_g, w_in, gate_b, cos, sin, dw_w, dw_b, ln_g, ln_b, w_cp, cp_b, ret_g)


def _retention_body(cdec_ref, q_ref, k_ref, v_ref, sg_ref, inner_ref, cross_ref, state_ref,
                    o_ref, r_scr, *, chunks, unroll):
    r_scr[...] = jnp.zeros_like(r_scr)
    chunk_decay = cdec_ref[pl.program_id(1)]
    c_len = RET_CHUNK

    def chunk_group(i, carry):
        base = pl.multiple_of(i * (unroll * c_len), unroll * c_len)
        for c in range(unroll):
            rows = pl.ds(base + c * c_len, c_len)
            qc = q_ref[rows, :]
            kc = k_ref[rows, :]
            vc = v_ref[rows, :]
            state = r_scr[...]
            s = lax.dot_general(qc, kc, (((1,), (1,)), ((), ())), preferred_element_type=F32) * inner_ref[0]
            inner = jnp.dot(s.astype(BF16), vc, preferred_element_type=F32)
            cross = jnp.dot(qc, state.astype(BF16), preferred_element_type=F32) * cross_ref[0]
            kw = (kc.astype(F32) * state_ref[0]).astype(BF16)
            r_scr[...] = state * chunk_decay + lax.dot_general(
                kw, vc, (((0,), (0,)), ((), ())), preferred_element_type=F32)
            r = inner + cross
            mu = jnp.mean(r, axis=-1, keepdims=True)
            d = r - mu
            var = jnp.mean(d * d, axis=-1, keepdims=True)
            rn = d * lax.rsqrt(var + EPS)
            o_ref[rows, :] = sg_ref[rows, :] * rn.astype(BF16)
        return carry

    lax.fori_loop(0, chunks // unroll, chunk_group, 0)


def _retention(mixed, *, batch, seq, heads, d_model, unroll=4):
    tokens = mixed.shape[0]
    c_len = RET_CHUNK
    assert seq % (unroll * c_len) == 0
    (q0, _), (k0, _), (v0, _), (g0, _) = _mixer_sections(d_model, heads)[:4]
    assert q0 % RET_DK == k0 % RET_DK == v0 % RET_DV == g0 % RET_DV == 0
    head_block = lambda col0, width: (lambda b, h: (b, col0 // width + h))
    log_gamma = jnp.log1p(-jnp.power(2.0, -5.0 - jnp.arange(heads, dtype=F32)))
    idx = jnp.arange(c_len, dtype=F32)
    diff = idx[:, None] - idx[None, :]
    causal = diff >= 0
    inner_decay = jnp.where(
        causal[None], jnp.exp(jnp.where(causal, diff, 0.0)[None] * log_gamma[:, None, None]), 0.0)
    cross_decay = jnp.exp((idx + 1.0)[None, :] * log_gamma[:, None])[:, :, None]
    state_decay = jnp.exp((c_len - 1.0 - idx)[None, :] * log_gamma[:, None])[:, :, None]
    chunk_decay = jnp.exp(c_len * log_gamma)

    seq_head = lambda b, h: (b, h)
    per_head = lambda b, h: (h, 0, 0)
    return pl.pallas_call(
        functools.partial(_retention_body, chunks=seq // c_len, unroll=unroll),
        grid=(batch, heads),
        in_specs=[
            pl.BlockSpec(memory_space=pltpu.SMEM),
            pl.BlockSpec((seq, RET_DK), head_block(q0, RET_DK)),
            pl.BlockSpec((seq, RET_DK), head_block(k0, RET_DK)),
            pl.BlockSpec((seq, RET_DV), head_block(v0, RET_DV)),
            pl.BlockSpec((seq, RET_DV), head_block(g0, RET_DV)),
            pl.BlockSpec((1, c_len, c_len), per_head),
            pl.BlockSpec((1, c_len, 1), per_head),
            pl.BlockSpec((1, c_len, 1), per_head),
        ],
        out_specs=pl.BlockSpec((seq, RET_DV), seq_head),
        out_shape=jax.ShapeDtypeStruct((tokens, heads * RET_DV), BF16),
        scratch_shapes=[pltpu.VMEM((RET_DK, RET_DV), F32)],
        compiler_params=pltpu.CompilerParams(
            dimension_semantics=("arbitrary", "arbitrary"), vmem_limit_bytes=VMEM_LIMIT),
        name="retention",
    )(chunk_decay, mixed, mixed, mixed, mixed, inner_decay, cross_decay, state_decay)


FFN_HALO = SUBLANES


def _post_body(x_ref, r_ref, gya_ref, gtb_ref, wrp_ref, wout_ref, ng_ref, wup_ref, dww_ref, dwb_ref,
               wdn_ref, fg_ref, o_ref, x1_scr, h_scr, carry, ubuf, ff_scr,
               *, tm, ffn_dim, fc, row_block, final_norm):
    @pl.when(pl.program_id(1) == 0)
    def _():
        carry[...] = jnp.zeros_like(carry)

    y_b = jnp.dot(r_ref[...], wrp_ref[...], preferred_element_type=F32)
    mix = (gya_ref[...].astype(F32) + gtb_ref[...].astype(F32) * y_b).astype(BF16)
    x1 = x_ref[...] + jnp.dot(mix, wout_ref[...], preferred_element_type=F32)
    x1_scr[...] = x1
    h_scr[...] = _rms_norm(x1, ng_ref[...]).astype(BF16)

    for j, c in enumerate(range(0, ffn_dim, fc)):
        for half_idx, off in enumerate((c, ffn_dim + c)):
            ub = ubuf.at[j % 2, half_idx]
            ci = 2 * j + half_idx
            ub[:, 0:2 * FFN_HALO, :] = carry[ci]
            _conv_buf_store(ub, FFN_HALO,
                            jnp.dot(h_scr[...], wup_ref[:, off:off + fc], preferred_element_type=F32))
            carry[ci] = ub[:, 2 * tm:2 * (tm + FFN_HALO), :]
        for r0 in range(0, tm, row_block):
            for s in range(fc // LANES):
                u_act, u_lin = (
                    _conv_block(ubuf.at[j % 2, half_idx], FFN_HALO, dww_ref, dwb_ref, r0, row_block, s,
                                off + s * LANES)
                    for half_idx, off in enumerate((c, ffn_dim + c)))
                ff_scr[r0:r0 + row_block, c + s * LANES:c + (s + 1) * LANES] = (
                    _silu(u_act) * u_lin).astype(BF16)

    y = x1_scr[...] + jnp.dot(ff_scr[...], wdn_ref[...], preferred_element_type=F32)
    if final_norm:
        y = _rms_norm(y, fg_ref[...])
    o_ref[...] = y


def _post(x2, r, mixed, w_rp, w_out, norm_g, w_up, dw_w, dw_b, w_down, final_g,
          *, batch, seq, heads, final_norm, tm=512, fc=512):
    tokens, d_model = x2.shape
    ffn_dim = w_down.shape[0]
    s_tiles = seq // tm
    assert dw_w.shape[0] - 1 <= FFN_HALO
    tok = lambda b, s: (b * s_tiles + s, 0)
    (gya0, _), (gtb0, _) = _mixer_sections(d_model, heads)[4:]
    assert gya0 % d_model == gtb0 % d_model == 0
    tok_section = lambda col0: (lambda b, s: (b * s_tiles + s, col0 // d_model))
    return pl.pallas_call(
        functools.partial(_post_body, tm=tm, ffn_dim=ffn_dim, fc=fc, row_block=32, final_norm=final_norm),
        grid=(batch, s_tiles),
        in_specs=[
            pl.BlockSpec((tm, d_model), tok),
            pl.BlockSpec((tm, r.shape[1]), tok),
            pl.BlockSpec((tm, d_model), tok_section(gya0)),
            pl.BlockSpec((tm, d_model), tok_section(gtb0)),
            _resident(w_rp.shape), _resident(w_out.shape),
            _resident(norm_g.shape), _resident(w_up.shape), _resident(dw_w.shape), _resident(dw_b.shape),
            _resident(w_down.shape), _resident(final_g.shape),
        ],
        out_specs=pl.BlockSpec((tm, d_model), tok),
        out_shape=jax.ShapeDtypeStruct((tokens, d_model), F32),
        scratch_shapes=[
            pltpu.VMEM((tm, d_model), F32),
            pltpu.VMEM((tm, d_model), BF16),
            pltpu.VMEM((2 * ffn_dim // fc,) + _conv_buf_shape(fc, FFN_HALO, 0), F32),
            pltpu.VMEM((2, 2) + _conv_buf_shape(fc, FFN_HALO, tm), F32),
            pltpu.VMEM((tm, ffn_dim), BF16),
        ],
        compiler_params=pltpu.CompilerParams(
            dimension_semantics=("arbitrary", "arbitrary"), vmem_limit_bytes=VMEM_LIMIT),
        name="post",
    )(x2, r, mixed, mixed, w_rp, w_out, norm_g, w_up, dw_w, dw_b, w_down, final_g)


def kernel(x, norm_mix_g, w_in, gate_b, conv_dw_w, conv_dw_b, conv_ln_g, conv_ln_b, w_conv_proj,
           conv_proj_b, ret_norm_g, w_ret_proj, w_out, norm_ffn_g, w_up, ffn_dw_w, ffn_dw_b, w_down,
           norm_final_g):
    batch, seq, d_model = x.shape
    depth = w_in.shape[0]
    heads = d_model // RET_DK
    half = RET_DK // 2

    pos = jnp.arange(seq, dtype=F32)
    inv_freq = ROPE_BASE ** (-jnp.arange(half, dtype=F32) / half)
    ang = pos[:, None] * inv_freq[None, :]
    cos, sin = jnp.cos(ang), jnp.sin(ang)

    row = lambda v: v.reshape(1, -1)
    x2 = x.reshape(batch * seq, d_model)
    for l in range(depth):
        mixed = _mixer_in(
            x2, row(norm_mix_g[l]), w_in[l].astype(BF16), row(gate_b[l]), cos, sin,
            conv_dw_w[l], row(conv_dw_b[l]), row(conv_ln_g[l]), row(conv_ln_b[l]),
            w_conv_proj[l].astype(BF16), row(conv_proj_b[l]), row(ret_norm_g[l]),
            batch=batch, seq=seq, heads=heads)
        r = _retention(mixed, batch=batch, seq=seq, heads=heads, d_model=d_model)
        x2 = _post(
            x2, r, mixed, w_ret_proj[l].astype(BF16), w_out[l].astype(BF16),
            row(norm_ffn_g[l]), w_up[l].astype(BF16), ffn_dw_w[l], row(ffn_dw_b[l]),
            w_down[l].astype(BF16), row(norm_final_g), batch=batch, seq=seq, heads=heads,
            final_norm=(l == depth - 1))
    return x2.reshape(batch, seq, d_model)
```
